```python
import math
import jax
import jax.numpy as jnp
from jax import lax
import numpy as np

D_MODEL = 1024
BATCH = 32
SEQ = 2048
DEPTH = 2

GRID_W = 64
CTX_LEN = 256
N_MOD = 6
EPS = 1e-6
NEG_INF = -1e30

D_MIX = D_MODEL
NA_HEAD_DIM = 64
D_NA = D_MIX // 2
NA_HEADS = D_NA // NA_HEAD_DIM
NA_WIN_H = 8
NA_WIN_W = 16
RPB_H = 2 * NA_WIN_H - 1
RPB_W = 2 * NA_WIN_W - 1
D_HY = D_MIX // 4
HY_ORDER = 2
HY_SHORT = 3
HY_EMB = 33
HY_FILT = 64
HY_FAST_DECAY = 0.3
HY_SLOW_DECAY = 1.5
HY_TARGET = 1e-2
D_CF = D_MIX - D_NA - D_HY
CF_KERNEL = 31
D_IN = 3 * D_NA + (HY_ORDER + 1) * D_HY + 2 * D_CF
N_EXPERTS = 16
EC_CAPACITY = 2
EXPERT_FF = 2 * D_MODEL

kernel_name = 'hybrid_na_hyena_conformer_ecmoe_dit'


def rmsnorm(x, g):
    x32 = x.astype(jnp.float32)
    y = x32 * lax.rsqrt(jnp.mean(x32 * x32, axis=-1, keepdims=True) + EPS)
    return (y * g.astype(jnp.float32)).astype(x.dtype)


def modulate(h, shift, scale):
    return h * (1 + scale) + shift


def adaln(cvec, w_mod, b_mod):
    m = jax.nn.silu(cvec) @ w_mod + b_mod
    return jnp.split(m, N_MOD, axis=-1)


def heads(t):
    return t.reshape(t.shape[0], t.shape[1], NA_HEADS, NA_HEAD_DIM)


def depthwise_conv(x, w, b):
    k = w.shape[0]
    pad = (k - 1) // 2
    y = lax.conv_general_dilated(
        x, w.astype(x.dtype)[:, None, :], window_strides=(1,), padding=[(pad, k - 1 - pad)],
        dimension_numbers=('NWC', 'WIO', 'NWC'), feature_group_count=x.shape[-1])
    return y + b.astype(x.dtype)


def neighbourhood_attention(q, k, v, k_ctx, v_ctx, rpb):
    B, S, H, Dh = q.shape
    rows = S // GRID_W
    wh = min(NA_WIN_H, rows)
    nk = wh * GRID_W
    scale = Dh ** -0.5
    qg = q.reshape(B, rows, GRID_W, H, Dh)
    kg = k.reshape(B, rows, GRID_W, H, Dh)
    vg = v.reshape(B, rows, GRID_W, H, Dh)
    col = jnp.arange(GRID_W)
    c0 = jnp.clip(col - NA_WIN_W // 2, 0, GRID_W - NA_WIN_W)
    col_ok = (col[None, :] >= c0[:, None]) & (col[None, :] < c0[:, None] + NA_WIN_W)
    mask = jnp.broadcast_to(col_ok[:, None, :], (GRID_W, wh, GRID_W)).reshape(GRID_W, nk)
    dc = jnp.clip(col[None, :] - col[:, None], 1 - NA_WIN_W, NA_WIN_W - 1) + NA_WIN_W - 1
    rpb32 = rpb.astype(jnp.float32)

    def row_block(r):
        r0 = jnp.clip(r - wh // 2, 0, rows - wh)
        qb = lax.dynamic_index_in_dim(qg, r, axis=1, keepdims=False)
        kb = lax.dynamic_slice_in_dim(kg, r0, wh, axis=1).reshape(B, nk, H, Dh)
        vb = lax.dynamic_slice_in_dim(vg, r0, wh, axis=1).reshape(B, nk, H, Dh)
        dr = r0 + jnp.arange(wh) - r + NA_WIN_H - 1
        bias = rpb32[:, dr[None, :, None], dc[:, None, :]].reshape(H, GRID_W, nk)
        s_nb = jnp.einsum('bqhd,bkhd->bhqk', qb, kb).astype(jnp.float32) * scale + bias
        s_nb = jnp.where(mask, s_nb, NEG_INF)
        s_cx = jnp.einsum('bqhd,bkhd->bhqk', qb, k_ctx).astype(jnp.float32) * scale
        p = jax.nn.softmax(jnp.concatenate([s_nb, s_cx], axis=-1), axis=-1).astype(v.dtype)
        return (jnp.einsum('bhqk,bkhd->bqhd', p[..., :nk], vb)
                + jnp.einsum('bhqk,bkhd->bqhd', p[..., nk:], v_ctx))

    out = lax.map(row_block, jnp.arange(rows))
    return out.transpose(1, 0, 2, 3, 4).reshape(B, S, H * Dh)


def context_attention(q, k, v):
    s = jnp.einsum('bqhd,bkhd->bhqk', q, k).astype(jnp.float32) * (q.shape[-1] ** -0.5)
    p = jax.nn.softmax(s, axis=-1).astype(v.dtype)
    o = jnp.einsum('bhqk,bkhd->bqhd', p, v)
    return o.reshape(o.shape[0], o.shape[1], -1)


def hyena_filter_spectrum(L, w1, b1, freq, w2, b2, w3):
    f32 = jnp.float32
    pos = jnp.arange(L, dtype=f32)[:, None]
    t = pos / max(L - 1, 1)
    bands = (HY_EMB - 1) // 2
    fb = jnp.linspace(1e-4, bands - 1, bands, dtype=f32)[None, :]
    ang = fb * (2.0 * math.pi) * pos / L
    z = jnp.concatenate([t, jnp.cos(ang), -jnp.sin(ang)], axis=-1)
    fr = freq.astype(f32)
    h = jnp.sin(fr * (z @ w1.astype(f32) + b1.astype(f32)))
    h = jnp.sin(fr * (h @ w2.astype(f32) + b2.astype(f32)))
    h = (h @ w3.astype(f32)).reshape(L, 2, HY_ORDER, D_HY)
    min_decay = math.log(HY_TARGET) / HY_SLOW_DECAY
    max_decay = math.log(HY_TARGET) / HY_FAST_DECAY
    rate = jnp.abs(jnp.linspace(min_decay, max_decay, D_HY, dtype=f32))
    h = h * jnp.exp(-t * rate)[:, None, None, :]
    fwd, bwd = h[:, 0], h[:, 1]
    k_circ = jnp.concatenate([fwd, jnp.zeros_like(fwd[:1]), bwd[1:][::-1]], axis=0)
    return jnp.fft.rfft(k_circ, axis=0)


def hyena_mixer(u, short_w, short_b, bias_d, w1, b1, freq, w2, b2, w3):
    dt = u.dtype
    L = u.shape[1]
    u = depthwise_conv(u, short_w, short_b).astype(jnp.float32)
    x1, x2, v = jnp.split(u, 3, axis=-1)
    k_spec = hyena_filter_spectrum(L, w1, b1, freq, w2, b2, w3)
    d = bias_d.astype(jnp.float32)
    z = v
    for n, gate in enumerate((x1, x2)):
        z_f = jnp.fft.rfft(z, n=2 * L, axis=1)
        conv = jnp.fft.irfft(z_f * k_spec[None, :, n], n=2 * L, axis=1)[:, :L]
        z = gate * (conv + d[n] * z)
    return z.astype(dt)


def conformer_conv(u, dw_w, dw_b, ln_g, ln_b):
    a, g = jnp.split(u, 2, axis=-1)
    y = depthwise_conv(a * jax.nn.sigmoid(g), dw_w, dw_b).astype(jnp.float32)
    mu = jnp.mean(y, axis=-1, keepdims=True)
    var = jnp.mean(jnp.square(y - mu), axis=-1, keepdims=True)
    y = (y - mu) * lax.rsqrt(var + EPS) * ln_g.astype(jnp.float32) + ln_b.astype(jnp.float32)
    return jax.nn.silu(y).astype(u.dtype)


def expert_choice_ffn(h, w_router, w1, w3, w2):
    B, L, _ = h.shape
    cap = EC_CAPACITY * L // N_EXPERTS
    aff = jax.nn.softmax((h @ w_router).astype(jnp.float32), axis=-1)
    gate, idx = lax.top_k(jnp.swapaxes(aff, 1, 2), cap)
    b_idx = jnp.arange(B)[:, None, None]
    xe = h[b_idx, idx]
    a = jnp.einsum('becd,edf->becf', xe, w1)
    u = jnp.einsum('becd,edf->becf', xe, w3)
    y = jnp.einsum('becf,efd->becd', jax.nn.silu(a) * u, w2)
    y = y * gate[..., None].astype(y.dtype)
    return jnp.zeros_like(h).at[b_idx, idx].add(y)


def setup_inputs(seed: int = 0) -> dict:
    key = jax.random.key(seed)
    ks = jax.random.split(key, 32)
    f32 = jnp.float32

    def nrm(k, shape, scale):
        return scale * jax.random.normal(k, shape, f32)

    L = DEPTH
    return {
        'x': nrm(ks[0], (BATCH, SEQ, D_MODEL), 1.0),
        'c': nrm(ks[1], (BATCH, D_MODEL), 1.0),
        'ctx': nrm(ks[2], (BATCH, CTX_LEN, D_MODEL), 1.0),
        'c_ctx': nrm(ks[3], (D_MODEL,), 1.0),
        'w_mod': nrm(ks[4], (L, D_MODEL, N_MOD * D_MODEL), 0.5 * D_MODEL ** -0.5),
        'b_mod': nrm(ks[5], (L, N_MOD * D_MODEL), 0.02),
        'norm1_g': 1.0 + nrm(ks[6], (L, D_MODEL), 0.02),
        'norm2_g': 1.0 + nrm(ks[7], (L, D_MODEL), 0.02),
        'w_in': nrm(ks[8], (L, D_MODEL, D_IN), D_MODEL ** -0.5),
        'na_rpb': nrm(ks[9], (L, NA_HEADS, RPB_H, RPB_W), 0.1),
        'hy_short_w': nrm(ks[10], (L, HY_SHORT, (HY_ORDER + 1) * D_HY), HY_SHORT ** -0.5),
        'hy_short_b': nrm(ks[11], (L, (HY_ORDER + 1) * D_HY), 0.02),
        'hy_filt_w1': nrm(ks[12], (L, HY_EMB, HY_FILT), HY_EMB ** -0.5),
        'hy_filt_b1': nrm(ks[13], (L, HY_FILT), 0.02),
        'hy_filt_freq': 1.0 + nrm(ks[14], (L, HY_FILT), 0.02),
        'hy_filt_w2': nrm(ks[15], (L, HY_FILT, HY_FILT), HY_FILT ** -0.5),
        'hy_filt_b2': nrm(ks[16], (L, HY_FILT), 0.02),
        'hy_filt_w3': nrm(ks[17], (L, HY_FILT, 2 * HY_ORDER * D_HY), 0.05 * HY_FILT ** -0.5),
        'hy_bias_d': nrm(ks[18], (L, HY_ORDER, D_HY), 0.5),
        'cf_dw_w': nrm(ks[19], (L, CF_KERNEL, D_CF), CF_KERNEL ** -0.5),
        'cf_dw_b': nrm(ks[20], (L, D_CF), 0.02),
        'cf_ln_g': 1.0 + nrm(ks[21], (L, D_CF), 0.02),
        'cf_ln_b': nrm(ks[22], (L, D_CF), 0.02),
        'w_out': nrm(ks[23], (L, D_MIX, D_MODEL), D_MIX ** -0.5),
        'router_w': nrm(ks[24], (L, D_MODEL, N_EXPERTS), D_MODEL ** -0.5),
        'expert_w1': nrm(ks[25], (L, N_EXPERTS, D_MODEL, EXPERT_FF), D_MODEL ** -0.5),
        'expert_w3': nrm(ks[26], (L, N_EXPERTS, D_MODEL, EXPERT_FF), D_MODEL ** -0.5),
        'expert_w2': nrm(ks[27], (L, N_EXPERTS, EXPERT_FF, D_MODEL), EXPERT_FF ** -0.5),
        'final_norm_g': 1.0 + nrm(ks[28], (D_MODEL,), 0.02),
    }


def reference(x, c, ctx, c_ctx, w_mod, b_mod, norm1_g, norm2_g, w_in, na_rpb,
              hy_short_w, hy_short_b, hy_filt_w1, hy_filt_b1, hy_filt_freq, hy_filt_w2,
              hy_filt_b2, hy_filt_w3, hy_bias_d, cf_dw_w, cf_dw_b, cf_ln_g, cf_ln_b,
              w_out, router_w, expert_w1, expert_w3, expert_w2, final_norm_g):
    splits = [D_NA, 2 * D_NA, 3 * D_NA, 3 * D_NA + (HY_ORDER + 1) * D_HY]
    xc = ctx
    for l in range(DEPTH):
        last = l == DEPTH - 1
        sh1, sc1, g1, sh2, sc2, g2 = [m[:, None, :] for m in adaln(c, w_mod[l], b_mod[l])]
        csh1, csc1, cg1, csh2, csc2, cg2 = adaln(c_ctx, w_mod[l], b_mod[l])

        def head_groups(attn, hy_in, cf_in):
            hy = hyena_mixer(hy_in, hy_short_w[l], hy_short_b[l], hy_bias_d[l], hy_filt_w1[l], hy_filt_b1[l],
                             hy_filt_freq[l], hy_filt_w2[l], hy_filt_b2[l], hy_filt_w3[l])
            cf = conformer_conv(cf_in, cf_dw_w[l], cf_dw_b[l], cf_ln_g[l], cf_ln_b[l])
            return jnp.concatenate([attn, hy, cf], axis=-1) @ w_out[l]

        h = modulate(rmsnorm(x, norm1_g[l]), sh1, sc1)
        q, k, v, hy_in, cf_in = jnp.split(h @ w_in[l], splits, axis=-1)
        hc = modulate(rmsnorm(xc, norm1_g[l]), csh1, csc1)
        if last:
            kc, vc = jnp.split(hc @ w_in[l][:, D_NA:3 * D_NA], 2, axis=-1)
        else:
            qc, kc, vc, hyc_in, cfc_in = jnp.split(hc @ w_in[l], splits, axis=-1)
        attn = neighbourhood_attention(heads(q), heads(k), heads(v), heads(kc), heads(vc), na_rpb[l])
        x = x + g1 * head_groups(attn, hy_in, cf_in)
        h2 = modulate(rmsnorm(x, norm2_g[l]), sh2, sc2)
        x = x + g2 * expert_choice_ffn(h2, router_w[l], expert_w1[l], expert_w3[l], expert_w2[l])

        if not last:
            attn_c = context_attention(heads(qc), heads(kc), heads(vc))
            xc = xc + cg1 * head_groups(attn_c, hyc_in, cfc_in)
            hc2 = modulate(rmsnorm(xc, norm2_g[l]), csh2, csc2)
            xc = xc + cg2 * expert_choice_ffn(hc2, router_w[l], expert_w1[l], expert_w3[l], expert_w2[l])
    return rmsnorm(x, final_norm_g)
```

```python
import functools
import math

import numpy as np
import jax
import jax.numpy as jnp
from jax import lax
from jax.experimental import pallas as pl
from jax.experimental.pallas import tpu as pltpu

F32, BF16, I32 = jnp.float32, jnp.bfloat16, jnp.int32
HIGHEST = lax.Precision.HIGHEST

D_MODEL = 1024
GRID_W = 64
N_MOD = 6
EPS = 1e-6
NEG_INF = -1e30
NA_HEAD_DIM = 64
D_NA = 512
NA_HEADS = 8
NA_WIN_H = 8
NA_WIN_W = 16
D_HY = 256
HY_EMB = 33
HY_FAST_DECAY = 0.3
HY_SLOW_DECAY = 1.5
HY_TARGET = 1e-2
D_CF = 256
CF_KERNEL = 31
D_QKV = 3 * D_NA
D_HYIN = 3 * D_HY
D_CFIN = 2 * D_CF
N_EXPERTS = 16
EC_CAPACITY = 2
EXPERT_FF = 2 * D_MODEL

NA_QROWS = 4
NA_KROWS = NA_QROWS + NA_WIN_H - 1
HY_T = 256
MIB = 1024 * 1024
VMEM_LIMIT = 56 * MIB


def _params(sem, vmem=VMEM_LIMIT):
    return pltpu.CompilerParams(dimension_semantics=sem, vmem_limit_bytes=vmem)


def _nt_dot(a, b):
    return lax.dot_general(a, b, (((1,), (1,)), ((), ())), preferred_element_type=F32)


def _tn_dot(a, b):
    return lax.dot_general(a, b, (((0,), (0,)), ((), ())), preferred_element_type=F32)


def _mod_kernel(c_ref, w_ref, b_ref, o_ref):
    c = c_ref[...]
    s = c * jax.nn.sigmoid(c)
    o_ref[0] = jnp.dot(s, w_ref[0], preferred_element_type=F32, precision=HIGHEST) + b_ref[0]


def _modulation(cvec, w_mod, b_mod):
    depth, _, n = w_mod.shape
    rows = cvec.shape[0]
    tn = 1536
    return pl.pallas_call(
        _mod_kernel,
        grid=(depth, n // tn),
        in_specs=[pl.BlockSpec((rows, D_MODEL), lambda l, j: (0, 0)),
                  pl.BlockSpec((1, D_MODEL, tn), lambda l, j: (l, 0, j)),
                  pl.BlockSpec((1, 1, tn), lambda l, j: (l, 0, j))],
        out_specs=pl.BlockSpec((1, rows, tn), lambda l, j: (l, 0, j)),
        out_shape=jax.ShapeDtypeStruct((depth, rows, n), F32),
        compiler_params=_params(("arbitrary", "arbitrary")),
        name="adaln_mod",
    )(cvec, w_mod, b_mod.reshape(depth, 1, n))


def _rms_mod(x, g, sc, sh):
    y = x * lax.rsqrt(jnp.mean(x * x, axis=-1, keepdims=True) + EPS)
    return (y * g) * (1.0 + sc) + sh


def _inproj_kernel(x_ref, g_ref, sc_ref, sh_ref, w_ref, qkv_ref, hy_ref, cf_ref):
    h = _rms_mod(x_ref[...], g_ref[...], sc_ref[0], sh_ref[0]).astype(BF16)
    qkv_ref[:, :D_NA] = (jnp.dot(h, w_ref[:, :D_NA], preferred_element_type=F32) * NA_HEAD_DIM ** -0.5).astype(BF16)
    qkv_ref[:, D_NA:] = jnp.dot(h, w_ref[:, D_NA:D_QKV], preferred_element_type=F32).astype(BF16)
    hy_ref[...] = jnp.dot(h, w_ref[:, D_QKV:D_QKV + D_HYIN], preferred_element_type=F32)
    cf_ref[...] = jnp.dot(h, w_ref[:, D_QKV + D_HYIN:], preferred_element_type=F32)


def _per_sample_spec(arr, tiles_per_sample):
    if arr.shape[0] == 1:
        return pl.BlockSpec((1, 1, D_MODEL), lambda i: (0, 0, 0))
    return pl.BlockSpec((1, 1, D_MODEL), lambda i: (i // tiles_per_sample, 0, 0))


def _in_projection(x2d, g, sc, sh, w_bf, seq, tm):
    n = x2d.shape[0]
    tps = seq // tm
    d_in = w_bf.shape[1]
    row = lambda w: pl.BlockSpec((tm, w), lambda i: (i, 0))
    return pl.pallas_call(
        _inproj_kernel,
        grid=(n // tm,),
        in_specs=[row(D_MODEL), pl.BlockSpec((1, D_MODEL), lambda i: (0, 0)),
                  _per_sample_spec(sc, tps), _per_sample_spec(sh, tps),
                  pl.BlockSpec((D_MODEL, d_in), lambda i: (0, 0))],
        out_specs=[row(D_QKV), row(D_HYIN), row(D_CFIN)],
        out_shape=[jax.ShapeDtypeStruct((n, D_QKV), BF16), jax.ShapeDtypeStruct((n, D_HYIN), F32),
                   jax.ShapeDtypeStruct((n, D_CFIN), F32)],
        compiler_params=_params(("parallel",)),
        name="in_projection",
    )(x2d, g, sc, sh, w_bf)


def _na_bias_tiles(rpb, rows):
    col = np.arange(GRID_W)
    c0 = np.clip(col - NA_WIN_W // 2, 0, GRID_W - NA_WIN_W)
    col_ok = (col[None, :] >= c0[:, None]) & (col[None, :] < c0[:, None] + NA_WIN_W)
    dc = np.clip(col[None, :] - col[:, None], 1 - NA_WIN_W, NA_WIN_W - 1) + NA_WIN_W - 1
    n_groups = rows // NA_QROWS
    dr_idx, ok = [], []
    for grp in (0, 1, n_groups - 1):
        r = grp * NA_QROWS + np.arange(NA_QROWS)
        k0 = np.clip(grp * NA_QROWS - NA_WIN_H // 2, 0, rows - NA_KROWS)
        kr = k0 + np.arange(NA_KROWS)
        r0 = np.clip(r - NA_WIN_H // 2, 0, rows - NA_WIN_H)
        row_ok = (kr[None, :] >= r0[:, None]) & (kr[None, :] < r0[:, None] + NA_WIN_H)
        dr = np.clip(kr[None, :] - r[:, None] + NA_WIN_H - 1, 0, 2 * NA_WIN_H - 2)
        dr_idx.append(np.broadcast_to(dr[:, None, :, None], (NA_QROWS, GRID_W, NA_KROWS, GRID_W)))
        ok.append(row_ok[:, None, :, None] & col_ok[None, :, None, :])
    nq, nk = NA_QROWS * GRID_W, NA_KROWS * GRID_W
    dr_idx = np.stack(dr_idx).reshape(3, nq, nk)
    ok = np.stack(ok).reshape(3, nq, nk)
    dc_idx = np.broadcast_to(dc[None, None, :, None, :], (3, NA_QROWS, GRID_W, NA_KROWS, GRID_W)).reshape(3, nq, nk)
    bias = rpb.astype(F32)[:, dr_idx, dc_idx]
    bias = jnp.where(jnp.asarray(ok)[None], bias, NEG_INF)
    return jnp.transpose(bias, (1, 0, 2, 3))


def _softmax_pv(s_list, v_list):
    m = functools.reduce(jnp.maximum, [jnp.max(s, axis=-1, keepdims=True) for s in s_list])
    p_list = [jnp.exp(s - m) for s in s_list]
    l = functools.reduce(jnp.add, [jnp.sum(p, axis=-1, keepdims=True) for p in p_list])
    o = functools.reduce(jnp.add, [jnp.dot(p.astype(BF16), v, preferred_element_type=F32)
                                   for p, v in zip(p_list, v_list)])
    return o / l


def _na_kernel(q_ref, k_ref, v_ref, kc_ref, vc_ref, bias_ref, o_ref, *, rows):
    g = pl.program_id(1)
    k0 = jnp.clip(g * NA_QROWS - NA_WIN_H // 2, 0, rows - NA_KROWS) * GRID_W
    k0 = pl.multiple_of(k0, GRID_W)
    nk = NA_KROWS * GRID_W
    for h in range(NA_HEADS):
        sl = slice(h * NA_HEAD_DIM, (h + 1) * NA_HEAD_DIM)
        qh = q_ref[:, sl]
        s_nb = _nt_dot(qh, k_ref[pl.ds(k0, nk), sl]) + bias_ref[0, h]
        s_cx = _nt_dot(qh, kc_ref[:, sl])
        o = _softmax_pv([s_nb, s_cx], [v_ref[pl.ds(k0, nk), sl], vc_ref[:, sl]])
        o_ref[:, sl] = o.astype(o_ref.dtype)


def _neighbourhood_attention(qkv, qkv_c, bias, batch, seq, ctx_len):
    rows = seq // GRID_W
    n_groups = rows // NA_QROWS
    nq, nk = NA_QROWS * GRID_W, NA_KROWS * GRID_W
    sel = lambda g: jnp.where(g == 0, 0, jnp.where(g == n_groups - 1, 2, 1))
    return pl.pallas_call(
        functools.partial(_na_kernel, rows=rows),
        grid=(batch, n_groups),
        in_specs=[pl.BlockSpec((nq, D_NA), lambda b, g: (b * n_groups + g, 0)),
                  pl.BlockSpec((seq, D_NA), lambda b, g: (b, 1)),
                  pl.BlockSpec((seq, D_NA), lambda b, g: (b, 2)),
                  pl.BlockSpec((ctx_len, D_NA), lambda b, g: (b, 1)),
                  pl.BlockSpec((ctx_len, D_NA), lambda b, g: (b, 2)),
                  pl.BlockSpec((1, NA_HEADS, nq, nk), lambda b, g: (sel(g), 0, 0, 0))],
        out_specs=pl.BlockSpec((nq, D_NA), lambda b, g: (b * n_groups + g, 0)),
        out_shape=jax.ShapeDtypeStruct((batch * seq, D_NA), BF16),
        compiler_params=_params(("parallel", "arbitrary")),
        name="neighbourhood_attention",
    )(qkv, qkv, qkv, qkv_c, qkv_c, bias)


def _ctx_attn_kernel(q_ref, k_ref, v_ref, o_ref):
    for h in range(NA_HEADS):
        sl = slice(h * NA_HEAD_DIM, (h + 1) * NA_HEAD_DIM)
        o = _softmax_pv([_nt_dot(q_ref[:, sl], k_ref[:, sl])], [v_ref[:, sl]])
        o_ref[:, sl] = o.astype(o_ref.dtype)


def _context_attention(qkv_c, batch, ctx_len):
    spec = lambda j: pl.BlockSpec((ctx_len, D_NA), lambda b: (b, j))
    return pl.pallas_call(
        _ctx_attn_kernel,
        grid=(batch,),
        in_specs=[spec(0), spec(1), spec(2)],
        out_specs=spec(0),
        out_shape=jax.ShapeDtypeStruct((batch * ctx_len, D_NA), BF16),
        compiler_params=_params(("parallel",)),
        name="context_attention",
    )(qkv_c, qkv_c, qkv_c)


CONV_CHUNK = 128
HY_HALO = 8
CF_HALO = 16


def _conv_kernel(hy_ref, cf_ref, sw_ref, sb_ref, dw_ref, db_ref, lg_ref, lb_ref,
                 x1_ref, x2_ref, v_ref, cfo_ref, hpad, gpad, *, seq):
    hpad[0:HY_HALO, :] = jnp.zeros((HY_HALO, D_HYIN), F32)
    hpad[seq + HY_HALO:seq + 2 * HY_HALO, :] = jnp.zeros((HY_HALO, D_HYIN), F32)
    hpad[HY_HALO:seq + HY_HALO, :] = hy_ref[...]
    gpad[0:CF_HALO, :] = jnp.zeros((CF_HALO, D_CF), F32)
    gpad[seq + CF_HALO:seq + 2 * CF_HALO, :] = jnp.zeros((CF_HALO, D_CF), F32)
    gpad[CF_HALO:seq + CF_HALO, :] = cf_ref[:, :D_CF] * jax.nn.sigmoid(cf_ref[:, D_CF:])
    ch = CONV_CHUNK
    pad_s = 1
    pad_c = (CF_KERNEL - 1) // 2

    def body(i, carry):
        base = pl.multiple_of(i * ch, ch)
        win = hpad[pl.ds(base, ch + 2 * HY_HALO), :]
        u = sb_ref[...]
        for t in range(3):
            o = HY_HALO - pad_s + t
            u = u + win[o:o + ch, :] * sw_ref[t:t + 1, :]
        x1_ref[0, :, pl.ds(base, ch)] = u[:, :D_HY].T
        x2_ref[0, :, pl.ds(base, ch)] = u[:, D_HY:2 * D_HY].T
        v_ref[0, :, pl.ds(base, ch)] = u[:, 2 * D_HY:].T

        gw = gpad[pl.ds(base, ch + 2 * CF_HALO), :]
        y = db_ref[...]
        for t in range(CF_KERNEL):
            o = CF_HALO - pad_c + t
            y = y + gw[o:o + ch, :] * dw_ref[t:t + 1, :]
        mu = jnp.mean(y, axis=-1, keepdims=True)
        yc = y - mu
        var = jnp.mean(yc * yc, axis=-1, keepdims=True)
        yn = yc * lax.rsqrt(var + EPS) * lg_ref[...] + lb_ref[...]
        cfo_ref[pl.ds(base, ch), :] = (yn * jax.nn.sigmoid(yn)).astype(cfo_ref.dtype)
        return carry

    lax.fori_loop(0, seq // ch, body, 0)


def _conv_mixers(hy_in, cf_in, sw, sb, dw, db, lg, lb, batch, seq):
    full = lambda a: pl.BlockSpec(a.shape, lambda b: (0, 0))
    cm = pl.BlockSpec((1, D_HY, seq), lambda b: (b, 0, 0))
    cm_shape = jax.ShapeDtypeStruct((batch, D_HY, seq), F32)
    return pl.pallas_call(
        functools.partial(_conv_kernel, seq=seq),
        grid=(batch,),
        in_specs=[pl.BlockSpec((seq, D_HYIN), lambda b: (b, 0)), pl.BlockSpec((seq, D_CFIN), lambda b: (b, 0)),
                  full(sw), full(sb), full(dw), full(db), full(lg), full(lb)],
        out_specs=[cm, cm, cm, pl.BlockSpec((seq, D_CF), lambda b: (b, 0))],
        out_shape=[cm_shape, cm_shape, cm_shape, jax.ShapeDtypeStruct((batch * seq, D_CF), BF16)],
        scratch_shapes=[pltpu.VMEM((seq + 2 * HY_HALO, D_HYIN), F32), pltpu.VMEM((seq + 2 * CF_HALO, D_CF), F32)],
        compiler_params=_params(("parallel",)),
        name="conv_mixers",
    )(hy_in, cf_in, sw, sb, dw, db, lg, lb)


def _filter_kernel(fb_ref, w1t_ref, w1c_ref, w1s_ref, b1_ref, fr_ref, w2_ref, b2_ref, w3_ref, rate_ref, o_ref, *, seq):
    pos = lax.broadcasted_iota(I32, (seq, 1), 0).astype(F32)
    t = pos / max(seq - 1, 1)
    ang = fb_ref[...] * (2.0 * math.pi) * pos / seq
    dot = functools.partial(jnp.dot, preferred_element_type=F32, precision=HIGHEST)
    fr = fr_ref[...]
    z1 = t * w1t_ref[...] + dot(jnp.cos(ang), w1c_ref[...]) + dot(-jnp.sin(ang), w1s_ref[...]) + b1_ref[...]
    h = jnp.sin(fr * z1)
    h = jnp.sin(fr * (dot(h, w2_ref[...]) + b2_ref[...]))
    o_ref[...] = dot(h, w3_ref[...]) * jnp.exp(-t * rate_ref[...])


def _hyena_filters(seq, w1, b1, freq, w2, b2, w3):
    bands = (HY_EMB - 1) // 2
    fb = jnp.linspace(1e-4, bands - 1, bands, dtype=F32)[None, :]
    min_decay = math.log(HY_TARGET) / HY_SLOW_DECAY
    max_decay = math.log(HY_TARGET) / HY_FAST_DECAY
    rate = jnp.abs(jnp.linspace(min_decay, max_decay, D_HY, dtype=F32))
    rate4 = jnp.tile(rate, 4)[None, :]
    args = (fb, w1[0:1], w1[1:1 + bands], w1[1 + bands:], b1[None, :], freq[None, :], w2, b2[None, :], w3, rate4)
    h = pl.pallas_call(
        functools.partial(_filter_kernel, seq=seq),
        out_shape=jax.ShapeDtypeStruct((seq, 4 * D_HY), F32),
        compiler_params=_params(None),
        name="hyena_filter",
    )(*args)
    h = h.reshape(seq, 2, 2, D_HY)
    fwd, bwd = h[:, 0], h[:, 1]
    line = jnp.concatenate([jnp.zeros_like(fwd[:1]), bwd[1:][::-1], fwd], axis=0)
    return jnp.transpose(line, (1, 2, 0))


HY_CG = 8


def _hyena_kernel(d_ref, x1_ref, x2_ref, v_ref, kf_ref, o_ref, w_scr, acc_scr, *, batch, seq):
    nb = seq // HY_T
    cg = pl.program_id(0)

    def blocks(ref, c):
        return jnp.concatenate([ref[:, c, j * HY_T:(j + 1) * HY_T] for j in range(nb)], axis=0)

    for c in range(HY_CG):
        z = blocks(v_ref, c)
        for n, gate_ref in enumerate((x1_ref, x2_ref)):
            line = jnp.broadcast_to(kf_ref[n, c:c + 1, :], (HY_T, 2 * seq))
            w_scr[...] = pltpu.roll(line, 0, 1, stride=1, stride_axis=0).astype(BF16)
            zb = z.astype(BF16)
            acc_scr[...] = jnp.dot(zb, w_scr[:, seq:seq + HY_T], preferred_element_type=F32)
            for d in range(1, nb):
                m = batch * (nb - d)
                acc_scr[batch * d:, :] += jnp.dot(zb[:m], w_scr[:, seq + d * HY_T:seq + (d + 1) * HY_T],
                                                  preferred_element_type=F32)
                acc_scr[:m, :] += jnp.dot(zb[batch * d:], w_scr[:, seq - d * HY_T:seq - (d - 1) * HY_T],
                                          preferred_element_type=F32)
            z = blocks(gate_ref, c) * (acc_scr[...] + d_ref[n, cg * HY_CG + c] * z)
        for j in range(nb):
            o_ref[:, c, j * HY_T:(j + 1) * HY_T] = z[j * batch:(j + 1) * batch]


def _hyena_long_conv(x1t, x2t, vt, kf, bias_d, batch, seq):
    cm = pl.BlockSpec((batch, HY_CG, seq), lambda g: (0, g, 0))
    return pl.pallas_call(
        functools.partial(_hyena_kernel, batch=batch, seq=seq),
        grid=(D_HY // HY_CG,),
        in_specs=[pl.BlockSpec(memory_space=pltpu.SMEM), cm, cm, cm,
                  pl.BlockSpec((2, HY_CG, 2 * seq), lambda g: (0, g, 0))],
        out_specs=cm,
        out_shape=jax.ShapeDtypeStruct((batch, D_HY, seq), F32),
        scratch_shapes=[pltpu.VMEM((HY_T, 2 * seq), BF16), pltpu.VMEM((seq // HY_T * batch, HY_T), F32)],
        compiler_params=_params(("parallel",)),
        name="hyena_long_conv",
    )(bias_d, x1t, x2t, vt, kf)


def _outproj_kernel(attn_ref, hy_ref, cf_ref, x_ref, g1_ref, w_ref, ng_ref, sc_ref, sh_ref, rw_ref,
                    xo_ref, h2_ref, lt_ref):
    y = jnp.dot(attn_ref[...], w_ref[:D_NA], preferred_element_type=F32)
    y = y + jnp.dot(hy_ref[0].T.astype(BF16), w_ref[D_NA:D_NA + D_HY], preferred_element_type=F32)
    y = y + jnp.dot(cf_ref[...], w_ref[D_NA + D_HY:], preferred_element_type=F32)
    xn = x_ref[...] + g1_ref[0] * y
    xo_ref[...] = xn
    h2 = _rms_mod(xn, ng_ref[...], sc_ref[0], sh_ref[0]).astype(BF16)
    h2_ref[...] = h2
    lt_ref[0] = _nt_dot(rw_ref[...], h2)


def _out_projection(attn, hyt, cf, x2d, g1, w_bf, ng, sc, sh, rw_t, batch, seq, tm):
    n = x2d.shape[0]
    tps = seq // tm
    row = lambda w: pl.BlockSpec((tm, w), lambda i: (i, 0))
    full = lambda a: pl.BlockSpec(a.shape, lambda i: (0, 0))
    return pl.pallas_call(
        _outproj_kernel,
        grid=(n // tm,),
        in_specs=[row(D_NA), pl.BlockSpec((1, D_HY, tm), lambda i: (i // tps, 0, i % tps)), row(D_CF), row(D_MODEL),
                  _per_sample_spec(g1, tps), full(w_bf), full(ng), _per_sample_spec(sc, tps),
                  _per_sample_spec(sh, tps), full(rw_t)],
        out_specs=[row(D_MODEL), row(D_MODEL), pl.BlockSpec((1, N_EXPERTS, tm), lambda i: (i // tps, 0, i % tps))],
        out_shape=[jax.ShapeDtypeStruct((n, D_MODEL), F32), jax.ShapeDtypeStruct((n, D_MODEL), BF16),
                   jax.ShapeDtypeStruct((batch, N_EXPERTS, seq), F32)],
        compiler_params=_params(("parallel",)),
        name="out_projection",
    )(attn, hyt, cf, x2d, g1, w_bf, ng, sc, sh, rw_t)


def _route_kernel(lt_ref, tri_ref, pos_ref, aff_ref, *, cap):
    lg = lt_ref[...]
    bs, _, seq = lg.shape
    e = jnp.exp(lg - jnp.max(lg, axis=1, keepdims=True))
    aff = e / jnp.sum(e, axis=1, keepdims=True)
    aff_ref[...] = aff
    keys = pltpu.bitcast(aff.reshape(bs * N_EXPERTS, seq), I32)

    def body(i, thr):
        cand = thr | (jnp.int32(1) << (30 - i))
        cnt = jnp.sum(jnp.where(keys >= cand, 1.0, 0.0), axis=1, keepdims=True)
        return jnp.where(cnt >= cap, cand, thr)

    thr = lax.fori_loop(0, 31, body, jnp.zeros((bs * N_EXPERTS, 1), I32))
    gt = keys > thr
    eq = keys == thr
    need = cap - jnp.sum(jnp.where(gt, 1.0, 0.0), axis=1, keepdims=True)
    rank = jnp.dot(jnp.where(eq, 1.0, 0.0).astype(BF16), tri_ref[...], preferred_element_type=F32)
    sel = gt | (eq & (rank < need))
    slot = jnp.dot(jnp.where(sel, 1.0, 0.0).astype(BF16), tri_ref[...], preferred_element_type=F32)
    pos_ref[...] = jnp.where(sel, slot, -1.0).astype(I32).reshape(bs, N_EXPERTS, seq)


def _routing(logits_t, tri, cap):
    batch, _, seq = logits_t.shape
    bs = 8 if batch % 8 == 0 else batch
    blk = pl.BlockSpec((bs, N_EXPERTS, seq), lambda i: (i, 0, 0))
    return pl.pallas_call(
        functools.partial(_route_kernel, cap=cap),
        grid=(batch // bs,),
        in_specs=[blk, pl.BlockSpec((seq, seq), lambda i: (0, 0))],
        out_specs=[blk, blk],
        out_shape=[jax.ShapeDtypeStruct((batch, N_EXPERTS, seq), I32),
                   jax.ShapeDtypeStruct((batch, N_EXPERTS, seq), F32)],
        compiler_params=_params(("parallel",)),
        name="ec_routing",
    )(logits_t, tri)


def _gather_kernel(pos_ref, h_ref, xe_ref, *, cap):
    seq = h_ref.shape[0]
    slot = lax.broadcasted_iota(I32, (cap, seq), 0)
    for e in range(N_EXPERTS):
        onehot = jnp.where(pos_ref[0, e:e + 1, :] == slot, 1.0, 0.0).astype(BF16)
        xe_ref[0, e * cap:(e + 1) * cap, :] = jnp.dot(onehot, h_ref[...], preferred_element_type=F32).astype(BF16)


def _gather_tokens(pos, h2, cap):
    batch, _, seq = pos.shape
    return pl.pallas_call(
        functools.partial(_gather_kernel, cap=cap),
        grid=(batch,),
        in_specs=[pl.BlockSpec((1, N_EXPERTS, seq), lambda b: (b, 0, 0)),
                  pl.BlockSpec((seq, D_MODEL), lambda b: (b, 0))],
        out_specs=pl.BlockSpec((1, N_EXPERTS * cap, D_MODEL), lambda b: (b, 0, 0)),
        out_shape=jax.ShapeDtypeStruct((batch, N_EXPERTS * cap, D_MODEL), BF16),
        compiler_params=_params(("parallel",)),
        name="ec_gather",
    )(pos, h2)


FFN_ROWS = 512


def _ffn_kernel(xe_ref, w1_ref, w3_ref, w2_ref, y_ref):
    bt, cap, _ = xe_ref.shape
    x = xe_ref[...].reshape(bt * cap, D_MODEL)
    a = jnp.dot(x, w1_ref[0], preferred_element_type=F32)
    u = jnp.dot(x, w3_ref[0], preferred_element_type=F32)
    h = (a * jax.nn.sigmoid(a) * u).astype(BF16)
    y_ref[...] = jnp.dot(h, w2_ref[0], preferred_element_type=F32).astype(BF16).reshape(bt, cap, D_MODEL)


def _expert_ffn(xe, w1_bf, w3_bf, w2_bf, cap):
    batch = xe.shape[0]
    bt = max(1, min(batch, FFN_ROWS // cap))
    while batch % bt:
        bt -= 1
    xspec = pl.BlockSpec((bt, cap, D_MODEL), lambda e, r: (r, e, 0))
    return pl.pallas_call(
        _ffn_kernel,
        grid=(N_EXPERTS, batch // bt),
        in_specs=[xspec, pl.BlockSpec((1, D_MODEL, EXPERT_FF), lambda e, r: (e, 0, 0)),
                  pl.BlockSpec((1, D_MODEL, EXPERT_FF), lambda e, r: (e, 0, 0)),
                  pl.BlockSpec((1, EXPERT_FF, D_MODEL), lambda e, r: (e, 0, 0))],
        out_specs=xspec,
        out_shape=jax.ShapeDtypeStruct(xe.shape, BF16),
        compiler_params=_params(("parallel", "arbitrary")),
        name="ec_expert_ffn",
    )(xe, w1_bf, w3_bf, w2_bf)


def _scatter_kernel(pos_ref, aff_ref, y_ref, x_ref, g2_ref, fg_ref, o_ref, pg_scr, *, cap, final_norm):
    tl = x_ref.shape[0]
    slot = lax.broadcasted_iota(I32, (cap, tl), 0)
    for e in range(N_EXPERTS):
        hit = pos_ref[0, e:e + 1, :] == slot
        pg_scr[e * cap:(e + 1) * cap, :] = jnp.where(hit, aff_ref[0, e:e + 1, :], 0.0).astype(BF16)
    moe = _tn_dot(pg_scr[...], y_ref[0])
    xn = x_ref[...] + g2_ref[0] * moe
    if final_norm:
        xn = xn * lax.rsqrt(jnp.mean(xn * xn, axis=-1, keepdims=True) + EPS) * fg_ref[...]
    o_ref[...] = xn


def _scatter_residual(pos, aff, y, x2d, g2, fg, cap, tl, final_norm):
    batch, _, seq = pos.shape
    tps = seq // tl
    rspec = pl.BlockSpec((1, N_EXPERTS, tl), lambda b, t: (b, 0, t))
    xspec = pl.BlockSpec((tl, D_MODEL), lambda b, t: (b * tps + t, 0))
    g2spec = (pl.BlockSpec((1, 1, D_MODEL), lambda b, t: (0, 0, 0)) if g2.shape[0] == 1
              else pl.BlockSpec((1, 1, D_MODEL), lambda b, t: (b, 0, 0)))
    return pl.pallas_call(
        functools.partial(_scatter_kernel, cap=cap, final_norm=final_norm),
        grid=(batch, tps),
        in_specs=[rspec, rspec, pl.BlockSpec((1, N_EXPERTS * cap, D_MODEL), lambda b, t: (b, 0, 0)), xspec,
                  g2spec, pl.BlockSpec((1, D_MODEL), lambda b, t: (0, 0))],
        out_specs=xspec,
        out_shape=jax.ShapeDtypeStruct(x2d.shape, F32),
        scratch_shapes=[pltpu.VMEM((N_EXPERTS * cap, tl), BF16)],
        compiler_params=_params(("parallel", "arbitrary")),
        name="ec_scatter_residual",
    )(pos, aff, y, x2d, g2, fg)


def _strict_lower_ones(n):
    i = np.arange(n)
    return jnp.asarray(i[:, None] < i[None, :], dtype=BF16)


def _stream_layer(x2d, mods, lw, batch, seq, tm, attn_fn, final_norm):
    sh1, sc1, g1, sh2, sc2, g2 = mods
    qkv, hy_in, cf_in = _in_projection(x2d, lw["norm1_g"], sc1, sh1, lw["w_in"], seq, tm)
    attn = attn_fn(qkv)
    x1t, x2t, vt, cf = _conv_mixers(hy_in, cf_in, lw["hy_short_w"], lw["hy_short_b"], lw["cf_dw_w"], lw["cf_dw_b"],
                                    lw["cf_ln_g"], lw["cf_ln_b"], batch, seq)
    kf = _hyena_filters(seq, lw["hy_filt_w1"], lw["hy_filt_b1"], lw["hy_filt_freq"], lw["hy_filt_w2"],
                        lw["hy_filt_b2"], lw["hy_filt_w3"])
    hyt = _hyena_long_conv(x1t, x2t, vt, kf, lw["hy_bias_d"], batch, seq)
    x_mid, h2, logits_t = _out_projection(attn, hyt, cf, x2d, g1, lw["w_out"], lw["norm2_g"], sc2, sh2,
                                          lw["router_wt"], batch, seq, tm)
    cap = EC_CAPACITY * seq // N_EXPERTS
    pos, aff = _routing(logits_t, _strict_lower_ones(seq), cap)
    xe = _gather_tokens(pos, h2, cap)
    y = _expert_ffn(xe, lw["expert_w1"], lw["expert_w3"], lw["expert_w2"], cap)
    return _scatter_residual(pos, aff, y, x_mid, g2, lw["final_norm_g"], cap, min(seq, 512), final_norm), qkv


def kernel(x, c, ctx, c_ctx, w_mod, b_mod, norm1_g, norm2_g, w_in, na_rpb, hy_short_w, hy_short_b, hy_filt_w1,
           hy_filt_b1, hy_filt_freq, hy_filt_w2, hy_filt_b2, hy_filt_w3, hy_bias_d, cf_dw_w, cf_dw_b, cf_ln_g,
           cf_ln_b, w_out, router_w, expert_w1, expert_w3, expert_w2, final_norm_g):
    batch, seq, _ = x.shape
    ctx_len = ctx.shape[1]
    depth = w_mod.shape[0]
    rows = seq // GRID_W

    n_c = batch + 1
    n_c_pad = -(-n_c // 8) * 8
    cvec = jnp.concatenate([c, c_ctx[None, :], jnp.zeros((n_c_pad - n_c, D_MODEL), F32)], axis=0)
    mod_all = _modulation(cvec, w_mod, b_mod)

    xl = x.reshape(batch * seq, D_MODEL)
    xc = ctx.reshape(batch * ctx_len, D_MODEL)
    for l in range(depth):
        last = l == depth - 1
        lw = dict(
            norm1_g=norm1_g[l][None, :], norm2_g=norm2_g[l][None, :], w_in=w_in[l].astype(BF16),
            hy_short_w=hy_short_w[l], hy_short_b=hy_short_b[l][None, :], hy_filt_w1=hy_filt_w1[l],
            hy_filt_b1=hy_filt_b1[l], hy_filt_freq=hy_filt_freq[l], hy_filt_w2=hy_filt_w2[l],
            hy_filt_b2=hy_filt_b2[l], hy_filt_w3=hy_filt_w3[l], hy_bias_d=hy_bias_d[l], cf_dw_w=cf_dw_w[l],
            cf_dw_b=cf_dw_b[l][None, :], cf_ln_g=cf_ln_g[l][None, :], cf_ln_b=cf_ln_b[l][None, :],
            w_out=w_out[l].astype(BF16), router_wt=router_w[l].T.astype(BF16),
            expert_w1=expert_w1[l].astype(BF16), expert_w3=expert_w3[l].astype(BF16),
            expert_w2=expert_w2[l].astype(BF16), final_norm_g=final_norm_g[None, :])
        chunks = [mod_all[l, :, j * D_MODEL:(j + 1) * D_MODEL] for j in range(N_MOD)]
        mods_l = [m[:batch, None, :] for m in chunks]
        mods_c = [m[batch:batch + 1, None, :] for m in chunks]
        bias = _na_bias_tiles(na_rpb[l], rows)

        if last:
            qkv_c, _, _ = _in_projection(xc, lw["norm1_g"], mods_c[1], mods_c[0], lw["w_in"], ctx_len, ctx_len)
        else:
            xc, qkv_c = _stream_layer(xc, mods_c, lw, batch, ctx_len, ctx_len,
                                      lambda qkv: _context_attention(qkv, batch, ctx_len), False)
        xl, _ = _stream_layer(xl, mods_l, lw, batch, seq, 512,
                              lambda qkv: _neighbourhood_attention(qkv, qkv_c, bias, batch, seq, ctx_len), last)
    return xl.reshape(batch, seq, D_MODEL)
```

```python
import functools
import math

import numpy as np
import jax
import jax.numpy as jnp
from jax import lax
from jax.experimental import pallas as pl
from jax.experimental.pallas import tpu as pltpu

F32, BF16, I32 = jnp.float32, jnp.bfloat16, jnp.int32
HIGHEST = lax.Precision.HIGHEST

D_MODEL = 1024
GRID_W = 64
N_MOD = 6
EPS = 1e-6
NEG_INF = -1e30
NA_HEAD_DIM = 64
D_NA = 512
NA_HEADS = 8
NA_WIN_H = 8
NA_WIN_W = 16
D_HY = 256
HY_EMB = 33
HY_FAST_DECAY = 0.3
HY_SLOW_DECAY = 1.5
HY_TARGET = 1e-2
D_CF = 256
CF_KERNEL = 31
D_QKV = 3 * D_NA
D_HYIN = 3 * D_HY
D_CFIN = 2 * D_CF
N_EXPERTS = 16
EC_CAPACITY = 2
EXPERT_FF = 2 * D_MODEL

NA_QROWS = 4
NA_KROWS = NA_QROWS + NA_WIN_H - 1
HY_T = 256
MIB = 1024 * 1024
VMEM_LIMIT = 56 * MIB


def _params(sem, vmem=VMEM_LIMIT):
    return pltpu.CompilerParams(dimension_semantics=sem, vmem_limit_bytes=vmem)


def _nt_dot(a, b):
    return lax.dot_general(a, b, (((1,), (1,)), ((), ())), preferred_element_type=F32)


def _tn_dot(a, b):
    return lax.dot_general(a, b, (((0,), (0,)), ((), ())), preferred_element_type=F32)


def _mod_kernel(c_ref, w_ref, b_ref, o_ref):
    c = c_ref[...]
    s = c * jax.nn.sigmoid(c)
    o_ref[0] = jnp.dot(s, w_ref[0], preferred_element_type=F32, precision=HIGHEST) + b_ref[0]


def _modulation(cvec, w_mod, b_mod):
    depth, _, n = w_mod.shape
    rows = cvec.shape[0]
    tn = 1536
    return pl.pallas_call(
        _mod_kernel,
        grid=(depth, n // tn),
        in_specs=[pl.BlockSpec((rows, D_MODEL), lambda l, j: (0, 0)),
                  pl.BlockSpec((1, D_MODEL, tn), lambda l, j: (l, 0, j)),
                  pl.BlockSpec((1, 1, tn), lambda l, j: (l, 0, j))],
        out_specs=pl.BlockSpec((1, rows, tn), lambda l, j: (l, 0, j)),
        out_shape=jax.ShapeDtypeStruct((depth, rows, n), F32),
        compiler_params=_params(("arbitrary", "arbitrary")),
        name="adaln_mod",
    )(cvec, w_mod, b_mod.reshape(depth, 1, n))


def _rms_mod(x, g, sc, sh):
    y = x * lax.rsqrt(jnp.mean(x * x, axis=-1, keepdims=True) + EPS)
    return (y * g) * (1.0 + sc) + sh


def _inproj_kernel(x_ref, g_ref, sc_ref, sh_ref, w_ref, qkv_ref, hy_ref, cf_ref):
    h = _rms_mod(x_ref[...], g_ref[...], sc_ref[0], sh_ref[0]).astype(BF16)
    qkv_ref[:, :D_NA] = (jnp.dot(h, w_ref[:, :D_NA], preferred_element_type=F32) * NA_HEAD_DIM ** -0.5).astype(BF16)
    qkv_ref[:, D_NA:] = jnp.dot(h, w_ref[:, D_NA:D_QKV], preferred_element_type=F32).astype(BF16)
    hy_ref[...] = jnp.dot(h, w_ref[:, D_QKV:D_QKV + D_HYIN], preferred_element_type=F32)
    cf_ref[...] = jnp.dot(h, w_ref[:, D_QKV + D_HYIN:], preferred_element_type=F32)


def _per_sample_spec(arr, tiles_per_sample):
    if arr.shape[0] == 1:
        return pl.BlockSpec((1, 1, D_MODEL), lambda i: (0, 0, 0))
    return pl.BlockSpec((1, 1, D_MODEL), lambda i: (i // tiles_per_sample, 0, 0))


def _in_projection(x2d, g, sc, sh, w_bf, seq, tm):
    n = x2d.shape[0]
    tps = seq // tm
    d_in = w_bf.shape[1]
    row = lambda w: pl.BlockSpec((tm, w), lambda i: (i, 0))
    return pl.pallas_call(
        _inproj_kernel,
        grid=(n // tm,),
        in_specs=[row(D_MODEL), pl.BlockSpec((1, D_MODEL), lambda i: (0, 0)),
                  _per_sample_spec(sc, tps), _per_sample_spec(sh, tps),
                  pl.BlockSpec((D_MODEL, d_in), lambda i: (0, 0))],
        out_specs=[row(D_QKV), row(D_HYIN), row(D_CFIN)],
        out_shape=[jax.ShapeDtypeStruct((n, D_QKV), BF16), jax.ShapeDtypeStruct((n, D_HYIN), F32),
                   jax.ShapeDtypeStruct((n, D_CFIN), F32)],
        compiler_params=_params(("parallel",)),
        name="in_projection",
    )(x2d, g, sc, sh, w_bf)


def _na_bias_tiles(rpb, rows):
    col = np.arange(GRID_W)
    c0 = np.clip(col - NA_WIN_W // 2, 0, GRID_W - NA_WIN_W)
    col_ok = (col[None, :] >= c0[:, None]) & (col[None, :] < c0[:, None] + NA_WIN_W)
    dc = np.clip(col[None, :] - col[:, None], 1 - NA_WIN_W, NA_WIN_W - 1) + NA_WIN_W - 1
    n_groups = rows // NA_QROWS
    n_dr, n_dc = 2 * NA_WIN_H - 1, 2 * NA_WIN_W - 1
    dr_hot = np.zeros((3, NA_QROWS, NA_KROWS, n_dr), np.float32)
    ok = np.zeros((3, NA_QROWS, GRID_W, NA_KROWS, GRID_W), bool)
    for v, grp in enumerate((0, 1, n_groups - 1)):
        r = grp * NA_QROWS + np.arange(NA_QROWS)
        k0 = np.clip(grp * NA_QROWS - NA_WIN_H // 2, 0, rows - NA_KROWS)
        kr = k0 + np.arange(NA_KROWS)
        r0 = np.clip(r - NA_WIN_H // 2, 0, rows - NA_WIN_H)
        row_ok = (kr[None, :] >= r0[:, None]) & (kr[None, :] < r0[:, None] + NA_WIN_H)
        dr = kr[None, :] - r[:, None] + NA_WIN_H - 1
        for i, j in zip(*np.nonzero(row_ok)):
            dr_hot[v, i, j, dr[i, j]] = 1.0
        ok[v] = row_ok[:, None, :, None] & col_ok[None, :, None, :]
    dc_hot = (dc[None, :, :] == np.arange(n_dc)[:, None, None]).astype(np.float32)
    t1 = jnp.einsum('hrd,dqk->hrqk', rpb.astype(F32), jnp.asarray(dc_hot), precision=HIGHEST)
    bias = jnp.einsum('vijr,hrqk->vhiqjk', jnp.asarray(dr_hot), t1, precision=HIGHEST)
    bias = jnp.where(jnp.asarray(ok)[:, None], bias, NEG_INF)
    return bias.reshape(3, NA_HEADS, NA_QROWS * GRID_W, NA_KROWS * GRID_W)


def _softmax_pv(s_list, v_list):
    m = functools.reduce(jnp.maximum, [jnp.max(s, axis=-1, keepdims=True) for s in s_list])
    p_list = [jnp.exp(s - m) for s in s_list]
    l = functools.reduce(jnp.add, [jnp.sum(p, axis=-1, keepdims=True) for p in p_list])
    o = functools.reduce(jnp.add, [jnp.dot(p.astype(BF16), v, preferred_element_type=F32)
                                   for p, v in zip(p_list, v_list)])
    return o / l


def _na_kernel(q_ref, k_ref, v_ref, kc_ref, vc_ref, bias_ref, o_ref, *, rows):
    g = pl.program_id(1)
    k0 = jnp.clip(g * NA_QROWS - NA_WIN_H // 2, 0, rows - NA_KROWS) * GRID_W
    k0 = pl.multiple_of(k0, GRID_W)
    nk = NA_KROWS * GRID_W
    for h in range(NA_HEADS):
        sl = slice(h * NA_HEAD_DIM, (h + 1) * NA_HEAD_DIM)
        qh = q_ref[:, sl]
        s_nb = _nt_dot(qh, k_ref[pl.ds(k0, nk), sl]) + bias_ref[0, h]
        s_cx = _nt_dot(qh, kc_ref[:, sl])
        o = _softmax_pv([s_nb, s_cx], [v_ref[pl.ds(k0, nk), sl], vc_ref[:, sl]])
        o_ref[:, sl] = o.astype(o_ref.dtype)


def _neighbourhood_attention(qkv, qkv_c, bias, batch, seq, ctx_len):
    rows = seq // GRID_W
    n_groups = rows // NA_QROWS
    nq, nk = NA_QROWS * GRID_W, NA_KROWS * GRID_W
    sel = lambda g: jnp.where(g == 0, 0, jnp.where(g == n_groups - 1, 2, 1))
    return pl.pallas_call(
        functools.partial(_na_kernel, rows=rows),
        grid=(batch, n_groups),
        in_specs=[pl.BlockSpec((nq, D_NA), lambda b, g: (b * n_groups + g, 0)),
                  pl.BlockSpec((seq, D_NA), lambda b, g: (b, 1)),
                  pl.BlockSpec((seq, D_NA), lambda b, g: (b, 2)),
                  pl.BlockSpec((ctx_len, D_NA), lambda b, g: (b, 1)),
                  pl.BlockSpec((ctx_len, D_NA), lambda b, g: (b, 2)),
                  pl.BlockSpec((1, NA_HEADS, nq, nk), lambda b, g: (sel(g), 0, 0, 0))],
        out_specs=pl.BlockSpec((nq, D_NA), lambda b, g: (b * n_groups + g, 0)),
        out_shape=jax.ShapeDtypeStruct((batch * seq, D_NA), BF16),
        compiler_params=_params(("parallel", "arbitrary")),
        name="neighbourhood_attention",
    )(qkv, qkv, qkv, qkv_c, qkv_c, bias)


def _ctx_attn_kernel(q_ref, k_ref, v_ref, o_ref):
    for h in range(NA_HEADS):
        sl = slice(h * NA_HEAD_DIM, (h + 1) * NA_HEAD_DIM)
        o = _softmax_pv([_nt_dot(q_ref[:, sl], k_ref[:, sl])], [v_ref[:, sl]])
        o_ref[:, sl] = o.astype(o_ref.dtype)


def _context_attention(qkv_c, batch, ctx_len):
    spec = lambda j: pl.BlockSpec((ctx_len, D_NA), lambda b: (b, j))
    return pl.pallas_call(
        _ctx_attn_kernel,
        grid=(batch,),
        in_specs=[spec(0), spec(1), spec(2)],
        out_specs=spec(0),
        out_shape=jax.ShapeDtypeStruct((batch * ctx_len, D_NA), BF16),
        compiler_params=_params(("parallel",)),
        name="context_attention",
    )(qkv_c, qkv_c, qkv_c)


SUBLANES = 8
SHORT_CHUNK = 128
DW_CHUNK = 128
DW_STAGES = 2
HY_HALO = 8
CF_HALO = 16


def _conv_kernel(hy_ref, cf_ref, sw_ref, sb_ref, dw_ref, db_ref, lg_ref, lb_ref,
                 x1_ref, x2_ref, v_ref, cfo_ref, hpad, gpad, ph_scr, *, seq):
    hpad[0:HY_HALO, :] = jnp.zeros((HY_HALO, D_HYIN), F32)
    hpad[seq + HY_HALO:seq + 2 * HY_HALO, :] = jnp.zeros((HY_HALO, D_HYIN), F32)
    hpad[HY_HALO:seq + HY_HALO, :] = hy_ref[...]
    gpad[0:CF_HALO, :] = jnp.zeros((CF_HALO, D_CF), F32)
    gpad[seq + CF_HALO:seq + 2 * CF_HALO, :] = jnp.zeros((CF_HALO, D_CF), F32)
    gpad[CF_HALO:seq + CF_HALO, :] = cf_ref[:, :D_CF] * jax.nn.sigmoid(cf_ref[:, D_CF:])
    pad_s = 1
    pad_c = (CF_KERNEL - 1) // 2

    def short_body(i, carry):
        ch = SHORT_CHUNK
        base = pl.multiple_of(i * ch, ch)
        win = hpad[pl.ds(base, ch + 2 * HY_HALO), :]
        u = sb_ref[...]
        for t in range(3):
            o = HY_HALO - pad_s + t
            u = u + win[o:o + ch, :] * sw_ref[t:t + 1, :]
        x1_ref[0, :, pl.ds(base, ch)] = u[:, :D_HY].T
        x2_ref[0, :, pl.ds(base, ch)] = u[:, D_HY:2 * D_HY].T
        v_ref[0, :, pl.ds(base, ch)] = u[:, 2 * D_HY:].T
        return carry

    lax.fori_loop(0, seq // SHORT_CHUNK, short_body, 0)

    def dw_chunk(base, stage):
        ch = DW_CHUNK
        gw = gpad[pl.ds(base, ch + 2 * CF_HALO), :]
        span = ch + 2 * CF_HALO - SUBLANES
        for phase in range(1, SUBLANES):
            ph_scr[stage, phase - 1] = gw[phase:phase + span, :]
        y = db_ref[...]
        for t in range(CF_KERNEL):
            o = CF_HALO - pad_c + t
            phase = o % SUBLANES
            if phase == 0:
                src = gpad[pl.ds(base + o, ch), :]
            else:
                src = ph_scr[stage, phase - 1, o - phase:o - phase + ch, :]
            y = y + src * dw_ref[t:t + 1, :]
        mu = jnp.mean(y, axis=-1, keepdims=True)
        yc = y - mu
        var = jnp.mean(yc * yc, axis=-1, keepdims=True)
        yn = yc * lax.rsqrt(var + EPS) * lg_ref[...] + lb_ref[...]
        cfo_ref[pl.ds(base, ch), :] = (yn * jax.nn.sigmoid(yn)).astype(cfo_ref.dtype)

    def dw_body(i, carry):
        for stage in range(DW_STAGES):
            dw_chunk(pl.multiple_of((i * DW_STAGES + stage) * DW_CHUNK, DW_CHUNK), stage)
        return carry

    lax.fori_loop(0, seq // (DW_CHUNK * DW_STAGES), dw_body, 0)


def _conv_mixers(hy_in, cf_in, sw, sb, dw, db, lg, lb, batch, seq):
    full = lambda a: pl.BlockSpec(a.shape, lambda b: (0, 0))
    cm = pl.BlockSpec((1, D_HY, seq), lambda b: (b, 0, 0))
    cm_shape = jax.ShapeDtypeStruct((batch, D_HY, seq), F32)
    return pl.pallas_call(
        functools.partial(_conv_kernel, seq=seq),
        grid=(batch,),
        in_specs=[pl.BlockSpec((seq, D_HYIN), lambda b: (b, 0)), pl.BlockSpec((seq, D_CFIN), lambda b: (b, 0)),
                  full(sw), full(sb), full(dw), full(db), full(lg), full(lb)],
        out_specs=[cm, cm, cm, pl.BlockSpec((seq, D_CF), lambda b: (b, 0))],
        out_shape=[cm_shape, cm_shape, cm_shape, jax.ShapeDtypeStruct((batch * seq, D_CF), BF16)],
        scratch_shapes=[pltpu.VMEM((seq + 2 * HY_HALO, D_HYIN), F32), pltpu.VMEM((seq + 2 * CF_HALO, D_CF), F32),
                        pltpu.VMEM((DW_STAGES, SUBLANES - 1, DW_CHUNK + 2 * CF_HALO - SUBLANES, D_CF), F32)],
        compiler_params=_params(("parallel",)),
        name="conv_mixers",
    )(hy_in, cf_in, sw, sb, dw, db, lg, lb)


def _filter_kernel(fb_ref, w1t_ref, w1c_ref, w1s_ref, b1_ref, fr_ref, w2_ref, b2_ref, w3_ref, rate_ref, o_ref, *, seq):
    pos = lax.broadcasted_iota(I32, (seq, 1), 0).astype(F32)
    t = pos / max(seq - 1, 1)
    ang = fb_ref[...] * (2.0 * math.pi) * pos / seq
    dot = functools.partial(jnp.dot, preferred_element_type=F32, precision=HIGHEST)
    fr = fr_ref[...]
    z1 = t * w1t_ref[...] + dot(jnp.cos(ang), w1c_ref[...]) + dot(-jnp.sin(ang), w1s_ref[...]) + b1_ref[...]
    h = jnp.sin(fr * z1)
    h = jnp.sin(fr * (dot(h, w2_ref[...]) + b2_ref[...]))
    o_ref[...] = dot(h, w3_ref[...]) * jnp.exp(-t * rate_ref[...])


def _hyena_filters(seq, w1, b1, freq, w2, b2, w3):
    bands = (HY_EMB - 1) // 2
    fb = jnp.linspace(1e-4, bands - 1, bands, dtype=F32)[None, :]
    min_decay = math.log(HY_TARGET) / HY_SLOW_DECAY
    max_decay = math.log(HY_TARGET) / HY_FAST_DECAY
    rate = jnp.abs(jnp.linspace(min_decay, max_decay, D_HY, dtype=F32))
    rate4 = jnp.tile(rate, 4)[None, :]
    args = (fb, w1[0:1], w1[1:1 + bands], w1[1 + bands:], b1[None, :], freq[None, :], w2, b2[None, :], w3, rate4)
    h = pl.pallas_call(
        functools.partial(_filter_kernel, seq=seq),
        out_shape=jax.ShapeDtypeStruct((seq, 4 * D_HY), F32),
        compiler_params=_params(None),
        name="hyena_filter",
    )(*args)
    h = h.reshape(seq, 2, 2, D_HY)
    fwd, bwd = h[:, 0], h[:, 1]
    line = jnp.concatenate([jnp.zeros_like(fwd[:1]), bwd[1:][::-1], fwd], axis=0)
    return jnp.transpose(line, (1, 2, 0))


HY_CG = 8


def _hyena_kernel(d_ref, x1_ref, x2_ref, v_ref, kf_ref, o_ref, w_scr, acc_scr, *, batch, seq):
    nb = seq // HY_T
    cg = pl.program_id(0)

    def blocks(ref, c):
        return jnp.concatenate([ref[:, c, j * HY_T:(j + 1) * HY_T] for j in range(nb)], axis=0)

    for c in range(HY_CG):
        z = blocks(v_ref, c)
        for n, gate_ref in enumerate((x1_ref, x2_ref)):
            line = jnp.broadcast_to(kf_ref[n, c:c + 1, :], (HY_T, 2 * seq))
            w_scr[...] = pltpu.roll(line, 0, 1, stride=1, stride_axis=0).astype(BF16)
            zb = z.astype(BF16)
            acc_scr[...] = jnp.dot(zb, w_scr[:, seq:seq + HY_T], preferred_element_type=F32)
            for d in range(1, nb):
                m = batch * (nb - d)
                acc_scr[batch * d:, :] += jnp.dot(zb[:m], w_scr[:, seq + d * HY_T:seq + (d + 1) * HY_T],
                                                  preferred_element_type=F32)
                acc_scr[:m, :] += jnp.dot(zb[batch * d:], w_scr[:, seq - d * HY_T:seq - (d - 1) * HY_T],
                                          preferred_element_type=F32)
            z = blocks(gate_ref, c) * (acc_scr[...] + d_ref[n, cg * HY_CG + c] * z)
        for j in range(nb):
            o_ref[:, c, j * HY_T:(j + 1) * HY_T] = z[j * batch:(j + 1) * batch]


def _hyena_long_conv(x1t, x2t, vt, kf, bias_d, batch, seq):
    cm = pl.BlockSpec((batch, HY_CG, seq), lambda g: (0, g, 0))
    return pl.pallas_call(
        functools.partial(_hyena_kernel, batch=batch, seq=seq),
        grid=(D_HY // HY_CG,),
        in_specs=[pl.BlockSpec(memory_space=pltpu.SMEM), cm, cm, cm,
                  pl.BlockSpec((2, HY_CG, 2 * seq), lambda g: (0, g, 0))],
        out_specs=cm,
        out_shape=jax.ShapeDtypeStruct((batch, D_HY, seq), F32),
        scratch_shapes=[pltpu.VMEM((HY_T, 2 * seq), BF16), pltpu.VMEM((seq // HY_T * batch, HY_T), F32)],
        compiler_params=_params(("parallel",)),
        name="hyena_long_conv",
    )(bias_d, x1t, x2t, vt, kf)


def _outproj_kernel(attn_ref, hy_ref, cf_ref, x_ref, g1_ref, w_ref, ng_ref, sc_ref, sh_ref, rw_ref,
                    xo_ref, h2_ref, lt_ref):
    y = jnp.dot(attn_ref[...], w_ref[:D_NA], preferred_element_type=F32)
    y = y + jnp.dot(hy_ref[0].T.astype(BF16), w_ref[D_NA:D_NA + D_HY], preferred_element_type=F32)
    y = y + jnp.dot(cf_ref[...], w_ref[D_NA + D_HY:], preferred_element_type=F32)
    xn = x_ref[...] + g1_ref[0] * y
    xo_ref[...] = xn
    h2 = _rms_mod(xn, ng_ref[...], sc_ref[0], sh_ref[0]).astype(BF16)
    h2_ref[...] = h2
    lt_ref[0] = _nt_dot(rw_ref[...], h2)


def _out_projection(attn, hyt, cf, x2d, g1, w_bf, ng, sc, sh, rw_t, batch, seq, tm):
    n = x2d.shape[0]
    tps = seq // tm
    row = lambda w: pl.BlockSpec((tm, w), lambda i: (i, 0))
    full = lambda a: pl.BlockSpec(a.shape, lambda i: (0, 0))
    return pl.pallas_call(
        _outproj_kernel,
        grid=(n // tm,),
        in_specs=[row(D_NA), pl.BlockSpec((1, D_HY, tm), lambda i: (i // tps, 0, i % tps)), row(D_CF), row(D_MODEL),
                  _per_sample_spec(g1, tps), full(w_bf), full(ng), _per_sample_spec(sc, tps),
                  _per_sample_spec(sh, tps), full(rw_t)],
        out_specs=[row(D_MODEL), row(D_MODEL), pl.BlockSpec((1, N_EXPERTS, tm), lambda i: (i // tps, 0, i % tps))],
        out_shape=[jax.ShapeDtypeStruct((n, D_MODEL), F32), jax.ShapeDtypeStruct((n, D_MODEL), BF16),
                   jax.ShapeDtypeStruct((batch, N_EXPERTS, seq), F32)],
        compiler_params=_params(("parallel",)),
        name="out_projection",
    )(attn, hyt, cf, x2d, g1, w_bf, ng, sc, sh, rw_t)


def _route_kernel(lt_ref, tri_ref, pos_ref, aff_ref, *, cap):
    lg = lt_ref[...]
    bs, _, seq = lg.shape
    e = jnp.exp(lg - jnp.max(lg, axis=1, keepdims=True))
    aff = e / jnp.sum(e, axis=1, keepdims=True)
    aff_ref[...] = aff
    keys = pltpu.bitcast(aff.reshape(bs * N_EXPERTS, seq), I32)

    def body(i, thr):
        cand = thr | (jnp.int32(1) << (30 - i))
        cnt = jnp.sum(jnp.where(keys >= cand, 1.0, 0.0), axis=1, keepdims=True)
        return jnp.where(cnt >= cap, cand, thr)

    thr = lax.fori_loop(0, 31, body, jnp.zeros((bs * N_EXPERTS, 1), I32))
    gt = keys > thr
    eq = keys == thr
    need = cap - jnp.sum(jnp.where(gt, 1.0, 0.0), axis=1, keepdims=True)
    rank = jnp.dot(jnp.where(eq, 1.0, 0.0).astype(BF16), tri_ref[...], preferred_element_type=F32)
    sel = gt | (eq & (rank < need))
    slot = jnp.dot(jnp.where(sel, 1.0, 0.0).astype(BF16), tri_ref[...], preferred_element_type=F32)
    pos_ref[...] = jnp.where(sel, slot, -1.0).astype(I32).reshape(bs, N_EXPERTS, seq)


def _routing(logits_t, tri, cap):
    batch, _, seq = logits_t.shape
    bs = 8 if batch % 8 == 0 else batch
    blk = pl.BlockSpec((bs, N_EXPERTS, seq), lambda i: (i, 0, 0))
    return pl.pallas_call(
        functools.partial(_route_kernel, cap=cap),
        grid=(batch // bs,),
        in_specs=[blk, pl.BlockSpec((seq, seq), lambda i: (0, 0))],
        out_specs=[blk, blk],
        out_shape=[jax.ShapeDtypeStruct((batch, N_EXPERTS, seq), I32),
                   jax.ShapeDtypeStruct((batch, N_EXPERTS, seq), F32)],
        compiler_params=_params(("parallel",)),
        name="ec_routing",
    )(logits_t, tri)


def _gather_kernel(pos_ref, h_ref, xe_ref, *, cap):
    seq = h_ref.shape[0]
    slot = lax.broadcasted_iota(I32, (cap, seq), 0)
    for e in range(N_EXPERTS):
        onehot = jnp.where(pos_ref[0, e:e + 1, :] == slot, 1.0, 0.0).astype(BF16)
        xe_ref[0, e * cap:(e + 1) * cap, :] = jnp.dot(onehot, h_ref[...], preferred_element_type=F32).astype(BF16)


def _gather_tokens(pos, h2, cap):
    batch, _, seq = pos.shape
    return pl.pallas_call(
        functools.partial(_gather_kernel, cap=cap),
        grid=(batch,),
        in_specs=[pl.BlockSpec((1, N_EXPERTS, seq), lambda b: (b, 0, 0)),
                  pl.BlockSpec((seq, D_MODEL), lambda b: (b, 0))],
        out_specs=pl.BlockSpec((1, N_EXPERTS * cap, D_MODEL), lambda b: (b, 0, 0)),
        out_shape=jax.ShapeDtypeStruct((batch, N_EXPERTS * cap, D_MODEL), BF16),
        compiler_params=_params(("parallel",)),
        name="ec_gather",
    )(pos, h2)


FFN_ROWS = 1024
FFN_TF = 512


def _ffn_kernel(xe_ref, w1_ref, w3_ref, w2_ref, y_ref, acc_ref):
    f = pl.program_id(2)
    bt, cap, _ = xe_ref.shape
    x = xe_ref[...].reshape(bt * cap, D_MODEL)
    a = jnp.dot(x, w1_ref[0, 0].astype(BF16), preferred_element_type=F32)
    u = jnp.dot(x, w3_ref[0, 0].astype(BF16), preferred_element_type=F32)
    h = (a * jax.nn.sigmoid(a) * u).astype(BF16)
    part = jnp.dot(h, w2_ref[0, 0].astype(BF16), preferred_element_type=F32)

    @pl.when(f == 0)
    def _():
        acc_ref[...] = part

    @pl.when(f > 0)
    def _():
        acc_ref[...] += part

    @pl.when(f == pl.num_programs(2) - 1)
    def _():
        y_ref[...] = acc_ref[...].astype(BF16).reshape(bt, cap, D_MODEL)


def _expert_ffn(xe, w1, w3, w2, layer, cap):
    batch = xe.shape[0]
    bt = max(1, min(batch, FFN_ROWS // cap))
    while batch % bt:
        bt -= 1
    xspec = pl.BlockSpec((bt, cap, D_MODEL), lambda e, r, f: (r, e, 0))
    return pl.pallas_call(
        _ffn_kernel,
        grid=(N_EXPERTS, batch // bt, EXPERT_FF // FFN_TF),
        in_specs=[xspec, pl.BlockSpec((1, 1, D_MODEL, FFN_TF), lambda e, r, f: (layer, e, 0, f)),
                  pl.BlockSpec((1, 1, D_MODEL, FFN_TF), lambda e, r, f: (layer, e, 0, f)),
                  pl.BlockSpec((1, 1, FFN_TF, D_MODEL), lambda e, r, f: (layer, e, f, 0))],
        out_specs=xspec,
        out_shape=jax.ShapeDtypeStruct(xe.shape, BF16),
        scratch_shapes=[pltpu.VMEM((bt * cap, D_MODEL), F32)],
        compiler_params=_params(("parallel", "arbitrary", "arbitrary")),
        name="ec_expert_ffn",
    )(xe, w1, w3, w2)


def _scatter_kernel(pos_ref, aff_ref, y_ref, x_ref, g2_ref, fg_ref, o_ref, pg_scr, *, cap, final_norm):
    tl = x_ref.shape[0]
    slot = lax.broadcasted_iota(I32, (cap, tl), 0)
    for e in range(N_EXPERTS):
        hit = pos_ref[0, e:e + 1, :] == slot
        pg_scr[e * cap:(e + 1) * cap, :] = jnp.where(hit, aff_ref[0, e:e + 1, :], 0.0).astype(BF16)
    moe = _tn_dot(pg_scr[...], y_ref[0])
    xn = x_ref[...] + g2_ref[0] * moe
    if final_norm:
        xn = xn * lax.rsqrt(jnp.mean(xn * xn, axis=-1, keepdims=True) + EPS) * fg_ref[...]
    o_ref[...] = xn


def _scatter_residual(pos, aff, y, x2d, g2, fg, cap, tl, final_norm):
    batch, _, seq = pos.shape
    tps = seq // tl
    rspec = pl.BlockSpec((1, N_EXPERTS, tl), lambda b, t: (b, 0, t))
    xspec = pl.BlockSpec((tl, D_MODEL), lambda b, t: (b * tps + t, 0))
    g2spec = (pl.BlockSpec((1, 1, D_MODEL), lambda b, t: (0, 0, 0)) if g2.shape[0] == 1
              else pl.BlockSpec((1, 1, D_MODEL), lambda b, t: (b, 0, 0)))
    return pl.pallas_call(
        functools.partial(_scatter_kernel, cap=cap, final_norm=final_norm),
        grid=(batch, tps),
        in_specs=[rspec, rspec, pl.BlockSpec((1, N_EXPERTS * cap, D_MODEL), lambda b, t: (b, 0, 0)), xspec,
                  g2spec, pl.BlockSpec((1, D_MODEL), lambda b, t: (0, 0))],
        out_specs=xspec,
        out_shape=jax.ShapeDtypeStruct(x2d.shape, F32),
        scratch_shapes=[pltpu.VMEM((N_EXPERTS * cap, tl), BF16)],
        compiler_params=_params(("parallel", "arbitrary")),
        name="ec_scatter_residual",
    )(pos, aff, y, x2d, g2, fg)


def _strict_lower_ones(n):
    i = np.arange(n)
    return jnp.asarray(i[:, None] < i[None, :], dtype=BF16)


def _stream_layer(x2d, mods, lw, batch, seq, tm, attn_fn, final_norm):
    sh1, sc1, g1, sh2, sc2, g2 = mods
    qkv, hy_in, cf_in = _in_projection(x2d, lw["norm1_g"], sc1, sh1, lw["w_in"], seq, tm)
    attn = attn_fn(qkv)
    x1t, x2t, vt, cf = _conv_mixers(hy_in, cf_in, lw["hy_short_w"], lw["hy_short_b"], lw["cf_dw_w"], lw["cf_dw_b"],
                                    lw["cf_ln_g"], lw["cf_ln_b"], batch, seq)
    kf = _hyena_filters(seq, lw["hy_filt_w1"], lw["hy_filt_b1"], lw["hy_filt_freq"], lw["hy_filt_w2"],
                        lw["hy_filt_b2"], lw["hy_filt_w3"])
    hyt = _hyena_long_conv(x1t, x2t, vt, kf, lw["hy_bias_d"], batch, seq)
    x_mid, h2, logits_t = _out_projection(attn, hyt, cf, x2d, g1, lw["w_out"], lw["norm2_g"], sc2, sh2,
                                          lw["router_wt"], batch, seq, tm)
    cap = EC_CAPACITY * seq // N_EXPERTS
    pos, aff = _routing(logits_t, _strict_lower_ones(seq), cap)
    xe = _gather_tokens(pos, h2, cap)
    y = _expert_ffn(xe, lw["expert_w1"], lw["expert_w3"], lw["expert_w2"], lw["layer"], cap)
    return _scatter_residual(pos, aff, y, x_mid, g2, lw["final_norm_g"], cap, min(seq, 512), final_norm), qkv


def kernel(x, c, ctx, c_ctx, w_mod, b_mod, norm1_g, norm2_g, w_in, na_rpb, hy_short_w, hy_short_b, hy_filt_w1,
           hy_filt_b1, hy_filt_freq, hy_filt_w2, hy_filt_b2, hy_filt_w3, hy_bias_d, cf_dw_w, cf_dw_b, cf_ln_g,
           cf_ln_b, w_out, router_w, expert_w1, expert_w3, expert_w2, final_norm_g):
    batch, seq, _ = x.shape
    ctx_len = ctx.shape[1]
    depth = w_mod.shape[0]
    rows = seq // GRID_W

    n_c = batch + 1
    n_c_pad = -(-n_c // 8) * 8
    cvec = jnp.concatenate([c, c_ctx[None, :], jnp.zeros((n_c_pad - n_c, D_MODEL), F32)], axis=0)
    mod_all = _modulation(cvec, w_mod, b_mod)

    xl = x.reshape(batch * seq, D_MODEL)
    xc = ctx.reshape(batch * ctx_len, D_MODEL)
    for l in range(depth):
        last = l == depth - 1
        lw = dict(
            norm1_g=norm1_g[l][None, :], norm2_g=norm2_g[l][None, :], w_in=w_in[l].astype(BF16),
            hy_short_w=hy_short_w[l], hy_short_b=hy_short_b[l][None, :], hy_filt_w1=hy_filt_w1[l],
            hy_filt_b1=hy_filt_b1[l], hy_filt_freq=hy_filt_freq[l], hy_filt_w2=hy_filt_w2[l],
            hy_filt_b2=hy_filt_b2[l], hy_filt_w3=hy_filt_w3[l], hy_bias_d=hy_bias_d[l], cf_dw_w=cf_dw_w[l],
            cf_dw_b=cf_dw_b[l][None, :], cf_ln_g=cf_ln_g[l][None, :], cf_ln_b=cf_ln_b[l][None, :],
            w_out=w_out[l].astype(BF16), router_wt=router_w[l].T.astype(BF16),
            expert_w1=expert_w1, expert_w3=expert_w3, expert_w2=expert_w2, layer=l,
            final_norm_g=final_norm_g[None, :])
        chunks = [mod_all[l, :, j * D_MODEL:(j + 1) * D_MODEL] for j in range(N_MOD)]
        mods_l = [m[:batch, None, :] for m in chunks]
        mods_c = [m[batch:batch + 1, None, :] for m in chunks]
        bias = _na_bias_tiles(na_rpb[l], rows)

        if last:
            qkv_c, _, _ = _in_projection(xc, lw["norm1_g"], mods_c[1], mods_c[0], lw["w_in"], ctx_len, ctx_len)
        else:
            xc, qkv_c = _stream_layer(xc, mods_c, lw, batch, ctx_len, ctx_len,
                                      lambda qkv: _context_attention(qkv, batch, ctx_len), False)
        xl, _ = _stream_layer(xl, mods_l, lw, batch, seq, 512,
                              lambda qkv: _neighbourhood_attention(qkv, qkv_c, bias, batch, seq, ctx_len), last)
    return xl.reshape(batch, seq, D_MODEL)
```

```python
import functools
import math

import numpy as np
import jax
import jax.numpy as jnp
from jax import lax
from jax.experimental import pallas as pl
from jax.experimental.pallas import tpu as pltpu

F32, BF16, I32 = jnp.float32, jnp.bfloat16, jnp.int32
HIGHEST = lax.Precision.HIGHEST

D_MODEL = 1024
GRID_W = 64
N_MOD = 6
EPS = 1e-6
NEG_INF = -1e30
NA_HEAD_DIM = 64
D_NA = 512
NA_HEADS = 8
NA_WIN_H = 8
NA_WIN_W = 16
D_HY = 256
HY_EMB = 33
HY_FAST_DECAY = 0.3
HY_SLOW_DECAY = 1.5
HY_TARGET = 1e-2
D_CF = 256
CF_KERNEL = 31
D_QKV = 3 * D_NA
V_SLOT = 128
D_QKVX = 2 * D_NA + NA_HEADS * V_SLOT
Q_SCALE = NA_HEAD_DIM ** -0.5 * math.log2(math.e)
D_HYIN = 3 * D_HY
D_CFIN = 2 * D_CF
N_EXPERTS = 16
EC_CAPACITY = 2
EXPERT_FF = 2 * D_MODEL

NA_QROWS = 4
NA_KROWS = NA_QROWS + NA_WIN_H - 1
HY_T = 256
MIB = 1024 * 1024
VMEM_LIMIT = 56 * MIB


def _params(sem, vmem=VMEM_LIMIT):
    return pltpu.CompilerParams(dimension_semantics=sem, vmem_limit_bytes=vmem)


def _nt_dot(a, b):
    return lax.dot_general(a, b, (((1,), (1,)), ((), ())), preferred_element_type=F32)


def _tn_dot(a, b):
    return lax.dot_general(a, b, (((0,), (0,)), ((), ())), preferred_element_type=F32)


def _mod_kernel(c_ref, w_ref, b_ref, o_ref):
    c = c_ref[...]
    s = c * jax.nn.sigmoid(c)
    o_ref[0] = jnp.dot(s, w_ref[0], preferred_element_type=F32, precision=HIGHEST) + b_ref[0]


def _modulation(cvec, w_mod, b_mod):
    depth, _, n = w_mod.shape
    rows = cvec.shape[0]
    tn = 1536
    return pl.pallas_call(
        _mod_kernel,
        grid=(depth, n // tn),
        in_specs=[pl.BlockSpec((rows, D_MODEL), lambda l, j: (0, 0)),
                  pl.BlockSpec((1, D_MODEL, tn), lambda l, j: (l, 0, j)),
                  pl.BlockSpec((1, 1, tn), lambda l, j: (l, 0, j))],
        out_specs=pl.BlockSpec((1, rows, tn), lambda l, j: (l, 0, j)),
        out_shape=jax.ShapeDtypeStruct((depth, rows, n), F32),
        compiler_params=_params(("arbitrary", "arbitrary")),
        name="adaln_mod",
    )(cvec, w_mod, b_mod.reshape(depth, 1, n))


def _rms_mod(x, g, sc, sh):
    y = x * lax.rsqrt(jnp.mean(x * x, axis=-1, keepdims=True) + EPS)
    return (y * g) * (1.0 + sc) + sh


def _inproj_kernel(x_ref, g_ref, sc_ref, sh_ref, w_ref, qkv_ref, hy_ref, cf_ref):
    h = _rms_mod(x_ref[...], g_ref[...], sc_ref[0], sh_ref[0]).astype(BF16)
    qkv_ref[:, :D_NA] = (jnp.dot(h, w_ref[:, :D_NA], preferred_element_type=F32) * Q_SCALE).astype(BF16)
    qkv_ref[:, D_NA:2 * D_NA] = jnp.dot(h, w_ref[:, D_NA:2 * D_NA], preferred_element_type=F32).astype(BF16)
    v = jnp.dot(h, w_ref[:, 2 * D_NA:D_QKV], preferred_element_type=F32).astype(BF16)
    lane = lax.broadcasted_iota(I32, (v.shape[0], V_SLOT - NA_HEAD_DIM), 1)
    ones_col = jnp.where(lane == 0, 1.0, 0.0).astype(BF16)
    for hd in range(NA_HEADS):
        lo = 2 * D_NA + hd * V_SLOT
        qkv_ref[:, lo:lo + NA_HEAD_DIM] = v[:, hd * NA_HEAD_DIM:(hd + 1) * NA_HEAD_DIM]
        qkv_ref[:, lo + NA_HEAD_DIM:lo + V_SLOT] = ones_col
    hy_ref[...] = jnp.dot(h, w_ref[:, D_QKV:D_QKV + D_HYIN], preferred_element_type=F32)
    cf_ref[...] = jnp.dot(h, w_ref[:, D_QKV + D_HYIN:], preferred_element_type=F32)


def _per_sample_spec(arr, tiles_per_sample):
    if arr.shape[0] == 1:
        return pl.BlockSpec((1, 1, D_MODEL), lambda i: (0, 0, 0))
    return pl.BlockSpec((1, 1, D_MODEL), lambda i: (i // tiles_per_sample, 0, 0))


def _in_projection(x2d, g, sc, sh, w_bf, seq, tm):
    n = x2d.shape[0]
    tps = seq // tm
    d_in = w_bf.shape[1]
    row = lambda w: pl.BlockSpec((tm, w), lambda i: (i, 0))
    return pl.pallas_call(
        _inproj_kernel,
        grid=(n // tm,),
        in_specs=[row(D_MODEL), pl.BlockSpec((1, D_MODEL), lambda i: (0, 0)),
                  _per_sample_spec(sc, tps), _per_sample_spec(sh, tps),
                  pl.BlockSpec((D_MODEL, d_in), lambda i: (0, 0))],
        out_specs=[row(D_QKVX), row(D_HYIN), row(D_CFIN)],
        out_shape=[jax.ShapeDtypeStruct((n, D_QKVX), BF16), jax.ShapeDtypeStruct((n, D_HYIN), F32),
                   jax.ShapeDtypeStruct((n, D_CFIN), F32)],
        compiler_params=_params(("parallel",)),
        name="in_projection",
    )(x2d, g, sc, sh, w_bf)


def _na_bias_tiles(rpb, rows):
    col = np.arange(GRID_W)
    c0 = np.clip(col - NA_WIN_W // 2, 0, GRID_W - NA_WIN_W)
    col_ok = (col[None, :] >= c0[:, None]) & (col[None, :] < c0[:, None] + NA_WIN_W)
    dc = np.clip(col[None, :] - col[:, None], 1 - NA_WIN_W, NA_WIN_W - 1) + NA_WIN_W - 1
    n_groups = rows // NA_QROWS
    n_dr, n_dc = 2 * NA_WIN_H - 1, 2 * NA_WIN_W - 1
    dr_hot = np.zeros((3, NA_QROWS, NA_KROWS, n_dr), np.float32)
    ok = np.zeros((3, NA_QROWS, GRID_W, NA_KROWS, GRID_W), bool)
    for v, grp in enumerate((0, 1, n_groups - 1)):
        r = grp * NA_QROWS + np.arange(NA_QROWS)
        k0 = np.clip(grp * NA_QROWS - NA_WIN_H // 2, 0, rows - NA_KROWS)
        kr = k0 + np.arange(NA_KROWS)
        r0 = np.clip(r - NA_WIN_H // 2, 0, rows - NA_WIN_H)
        row_ok = (kr[None, :] >= r0[:, None]) & (kr[None, :] < r0[:, None] + NA_WIN_H)
        dr = kr[None, :] - r[:, None] + NA_WIN_H - 1
        for i, j in zip(*np.nonzero(row_ok)):
            dr_hot[v, i, j, dr[i, j]] = 1.0
        ok[v] = row_ok[:, None, :, None] & col_ok[None, :, None, :]
    dc_hot = (dc[None, :, :] == np.arange(n_dc)[:, None, None]).astype(np.float32)
    t1 = jnp.einsum('hrd,dqk->hrqk', rpb.astype(F32), jnp.asarray(dc_hot), precision=HIGHEST)
    bias = jnp.einsum('vijr,hrqk->vhiqjk', jnp.asarray(dr_hot), t1, precision=HIGHEST)
    bias = jnp.where(jnp.asarray(ok)[:, None], bias * math.log2(math.e), NEG_INF)
    return bias.reshape(3, NA_HEADS, NA_QROWS * GRID_W, NA_KROWS * GRID_W)


def _softmax_pv(s_list, v_list):
    m = functools.reduce(jnp.maximum, [jnp.max(s, axis=-1, keepdims=True) for s in s_list])
    o = functools.reduce(jnp.add, [jnp.dot(jnp.exp2(s - m).astype(BF16), v, preferred_element_type=F32)
                                   for s, v in zip(s_list, v_list)])
    return o[:, :NA_HEAD_DIM] / o[:, NA_HEAD_DIM:NA_HEAD_DIM + 1]


def _na_kernel(q_ref, k_ref, v_ref, kc_ref, vc_ref, bias_ref, o_ref, *, rows):
    g = pl.program_id(1)
    k0 = jnp.clip(g * NA_QROWS - NA_WIN_H // 2, 0, rows - NA_KROWS) * GRID_W
    k0 = pl.multiple_of(k0, GRID_W)
    nk = NA_KROWS * GRID_W
    def scores(h):
        sl = slice(h * NA_HEAD_DIM, (h + 1) * NA_HEAD_DIM)
        qh = q_ref[:, sl]
        return [_nt_dot(qh, k_ref[pl.ds(k0, nk), sl]) + bias_ref[0, h], _nt_dot(qh, kc_ref[:, sl])]

    s = scores(0)
    for h in range(NA_HEADS):
        s_next = scores(h + 1) if h + 1 < NA_HEADS else None
        vs = slice(h * V_SLOT, (h + 1) * V_SLOT)
        o = _softmax_pv(s, [v_ref[pl.ds(k0, nk), vs], vc_ref[:, vs]])
        o_ref[:, h * NA_HEAD_DIM:(h + 1) * NA_HEAD_DIM] = o.astype(o_ref.dtype)
        s = s_next


def _neighbourhood_attention(qkv, qkv_c, bias, batch, seq, ctx_len):
    rows = seq // GRID_W
    n_groups = rows // NA_QROWS
    nq, nk = NA_QROWS * GRID_W, NA_KROWS * GRID_W
    sel = lambda g: jnp.where(g == 0, 0, jnp.where(g == n_groups - 1, 2, 1))
    return pl.pallas_call(
        functools.partial(_na_kernel, rows=rows),
        grid=(batch, n_groups),
        in_specs=[pl.BlockSpec((nq, D_NA), lambda b, g: (b * n_groups + g, 0)),
                  pl.BlockSpec((seq, D_NA), lambda b, g: (b, 1)),
                  pl.BlockSpec((seq, NA_HEADS * V_SLOT), lambda b, g: (b, 1)),
                  pl.BlockSpec((ctx_len, D_NA), lambda b, g: (b, 1)),
                  pl.BlockSpec((ctx_len, NA_HEADS * V_SLOT), lambda b, g: (b, 1)),
                  pl.BlockSpec((1, NA_HEADS, nq, nk), lambda b, g: (sel(g), 0, 0, 0))],
        out_specs=pl.BlockSpec((nq, D_NA), lambda b, g: (b * n_groups + g, 0)),
        out_shape=jax.ShapeDtypeStruct((batch * seq, D_NA), BF16),
        compiler_params=_params(("parallel", "arbitrary")),
        name="neighbourhood_attention",
    )(qkv, qkv, qkv, qkv_c, qkv_c, bias)


def _ctx_attn_kernel(q_ref, k_ref, v_ref, o_ref):
    for h in range(NA_HEADS):
        sl = slice(h * NA_HEAD_DIM, (h + 1) * NA_HEAD_DIM)
        o = _softmax_pv([_nt_dot(q_ref[:, sl], k_ref[:, sl])], [v_ref[:, h * V_SLOT:(h + 1) * V_SLOT]])
        o_ref[:, sl] = o.astype(o_ref.dtype)


def _context_attention(qkv_c, batch, ctx_len):
    spec = lambda j: pl.BlockSpec((ctx_len, D_NA), lambda b: (b, j))
    return pl.pallas_call(
        _ctx_attn_kernel,
        grid=(batch,),
        in_specs=[spec(0), spec(1), pl.BlockSpec((ctx_len, NA_HEADS * V_SLOT), lambda b: (b, 1))],
        out_specs=spec(0),
        out_shape=jax.ShapeDtypeStruct((batch * ctx_len, D_NA), BF16),
        compiler_params=_params(("parallel",)),
        name="context_attention",
    )(qkv_c, qkv_c, qkv_c)


SUBLANES = 8
LANES = 128
SHORT_CHUNK = 128
DW_CHUNK = 128
DW_STAGES = 2
HY_HALO = 8
CF_HALO = 16


def _conv_kernel(hy_ref, cf_ref, sw_ref, sb_ref, dw_ref, db_ref, lg_ref, lb_ref,
                 x1_ref, x2_ref, v_ref, cfo_ref, hpad, gpad, ph_scr, *, seq):
    hpad[0:HY_HALO, :] = jnp.zeros((HY_HALO, D_HYIN), F32)
    hpad[seq + HY_HALO:seq + 2 * HY_HALO, :] = jnp.zeros((HY_HALO, D_HYIN), F32)
    hpad[HY_HALO:seq + HY_HALO, :] = hy_ref[...]
    gpad[0:CF_HALO, :] = jnp.zeros((CF_HALO, D_CF), F32)
    gpad[seq + CF_HALO:seq + 2 * CF_HALO, :] = jnp.zeros((CF_HALO, D_CF), F32)
    gpad[CF_HALO:seq + CF_HALO, :] = cf_ref[:, :D_CF] * jax.nn.sigmoid(cf_ref[:, D_CF:])
    pad_s = 1
    pad_c = (CF_KERNEL - 1) // 2

    def short_body(i, carry):
        ch = SHORT_CHUNK
        base = pl.multiple_of(i * ch, ch)
        win = hpad[pl.ds(base, ch + 2 * HY_HALO), :]
        u = sb_ref[...]
        for t in range(3):
            o = HY_HALO - pad_s + t
            u = u + win[o:o + ch, :] * sw_ref[t:t + 1, :]
        x1_ref[0, :, pl.ds(base, ch)] = u[:, :D_HY].T
        x2_ref[0, :, pl.ds(base, ch)] = u[:, D_HY:2 * D_HY].T
        v_ref[0, :, pl.ds(base, ch)] = u[:, 2 * D_HY:].T
        return carry

    lax.fori_loop(0, seq // SHORT_CHUNK, short_body, 0)

    def dw_chunk(base, stage):
        ch = DW_CHUNK
        gw = gpad[pl.ds(base, ch + 2 * CF_HALO), :]
        span = ch + 2 * CF_HALO - SUBLANES
        for phase in range(1, SUBLANES):
            ph_scr[stage, phase - 1] = gw[phase:phase + span, :]
        y = db_ref[...]
        for t in range(CF_KERNEL):
            o = CF_HALO - pad_c + t
            phase = o % SUBLANES
            if phase == 0:
                src = gpad[pl.ds(base + o, ch), :]
            else:
                src = ph_scr[stage, phase - 1, o - phase:o - phase + ch, :]
            y = y + src * dw_ref[t:t + 1, :]
        mu = jnp.mean(y, axis=-1, keepdims=True)
        yc = y - mu
        var = jnp.mean(yc * yc, axis=-1, keepdims=True)
        yn = yc * lax.rsqrt(var + EPS) * lg_ref[...] + lb_ref[...]
        cfo_ref[pl.ds(base, ch), :] = (yn * jax.nn.sigmoid(yn)).astype(cfo_ref.dtype)

    def dw_body(i, carry):
        for stage in range(DW_STAGES):
            dw_chunk(pl.multiple_of((i * DW_STAGES + stage) * DW_CHUNK, DW_CHUNK), stage)
        return carry

    lax.fori_loop(0, seq // (DW_CHUNK * DW_STAGES), dw_body, 0)


def _conv_mixers(hy_in, cf_in, sw, sb, dw, db, lg, lb, batch, seq):
    full = lambda a: pl.BlockSpec(a.shape, lambda b: (0, 0))
    cm = pl.BlockSpec((1, D_HY, seq), lambda b: (b, 0, 0))
    cm_shape = jax.ShapeDtypeStruct((batch, D_HY, seq), F32)
    return pl.pallas_call(
        functools.partial(_conv_kernel, seq=seq),
        grid=(batch,),
        in_specs=[pl.BlockSpec((seq, D_HYIN), lambda b: (b, 0)), pl.BlockSpec((seq, D_CFIN), lambda b: (b, 0)),
                  full(sw), full(sb), full(dw), full(db), full(lg), full(lb)],
        out_specs=[cm, cm, cm, pl.BlockSpec((seq, D_CF), lambda b: (b, 0))],
        out_shape=[cm_shape, cm_shape, cm_shape, jax.ShapeDtypeStruct((batch * seq, D_CF), BF16)],
        scratch_shapes=[pltpu.VMEM((seq + 2 * HY_HALO, D_HYIN), F32), pltpu.VMEM((seq + 2 * CF_HALO, D_CF), F32),
                        pltpu.VMEM((DW_STAGES, SUBLANES - 1, DW_CHUNK + 2 * CF_HALO - SUBLANES, D_CF), F32)],
        compiler_params=_params(("parallel",)),
        name="conv_mixers",
    )(hy_in, cf_in, sw, sb, dw, db, lg, lb)


def _filter_kernel(fb_ref, w1t_ref, w1c_ref, w1s_ref, b1_ref, fr_ref, w2_ref, b2_ref, w3_ref, rate_ref, o_ref, *, seq):
    pos = lax.broadcasted_iota(I32, (seq, 1), 0).astype(F32)
    t = pos / max(seq - 1, 1)
    ang = fb_ref[...] * (2.0 * math.pi) * pos / seq
    dot = functools.partial(jnp.dot, preferred_element_type=F32, precision=HIGHEST)
    fr = fr_ref[...]
    z1 = t * w1t_ref[...] + dot(jnp.cos(ang), w1c_ref[...]) + dot(-jnp.sin(ang), w1s_ref[...]) + b1_ref[...]
    h = jnp.sin(fr * z1)
    h = jnp.sin(fr * (dot(h, w2_ref[...]) + b2_ref[...]))
    o_ref[...] = dot(h, w3_ref[...]) * jnp.exp(-t * rate_ref[...])


def _hyena_filters(seq, w1, b1, freq, w2, b2, w3):
    bands = (HY_EMB - 1) // 2
    fb = jnp.linspace(1e-4, bands - 1, bands, dtype=F32)[None, :]
    min_decay = math.log(HY_TARGET) / HY_SLOW_DECAY
    max_decay = math.log(HY_TARGET) / HY_FAST_DECAY
    rate = jnp.abs(jnp.linspace(min_decay, max_decay, D_HY, dtype=F32))
    rate4 = jnp.tile(rate, 4)[None, :]
    args = (fb, w1[0:1], w1[1:1 + bands], w1[1 + bands:], b1[None, :], freq[None, :], w2, b2[None, :], w3, rate4)
    h = pl.pallas_call(
        functools.partial(_filter_kernel, seq=seq),
        out_shape=jax.ShapeDtypeStruct((seq, 4 * D_HY), F32),
        compiler_params=_params(None),
        name="hyena_filter",
    )(*args)
    h = h.reshape(seq, 2, 2, D_HY)
    fwd, bwd = h[:, 0], h[:, 1]
    line = jnp.concatenate([jnp.zeros_like(fwd[:1]), bwd[1:][::-1], fwd], axis=0)
    return jnp.transpose(line, (1, 2, 0))


HY_CG = 8


def _hyena_kernel(d_ref, x1_ref, x2_ref, v_ref, kf_ref, o_ref, w_scr, acc_scr, *, batch, seq):
    nb = seq // HY_T
    cg = pl.program_id(0)

    def blocks(ref, c):
        return jnp.concatenate([ref[:, c, j * HY_T:(j + 1) * HY_T] for j in range(nb)], axis=0)

    for c in range(HY_CG):
        z = blocks(v_ref, c)
        for n, gate_ref in enumerate((x1_ref, x2_ref)):
            line = jnp.broadcast_to(kf_ref[n, c:c + 1, :], (HY_T, 2 * seq))
            w_scr[...] = pltpu.roll(line, 0, 1, stride=1, stride_axis=0).astype(BF16)
            zb = z.astype(BF16)
            acc_scr[...] = jnp.dot(zb, w_scr[:, seq:seq + HY_T], preferred_element_type=F32)
            for d in range(1, nb):
                m = batch * (nb - d)
                acc_scr[batch * d:, :] += jnp.dot(zb[:m], w_scr[:, seq + d * HY_T:seq + (d + 1) * HY_T],
                                                  preferred_element_type=F32)
                acc_scr[:m, :] += jnp.dot(zb[batch * d:], w_scr[:, seq - d * HY_T:seq - (d - 1) * HY_T],
                                          preferred_element_type=F32)
            z = blocks(gate_ref, c) * (acc_scr[...] + d_ref[n, cg * HY_CG + c] * z)
        for j in range(nb):
            o_ref[:, c, j * HY_T:(j + 1) * HY_T] = z[j * batch:(j + 1) * batch]


def _hyena_long_conv(x1t, x2t, vt, kf, bias_d, batch, seq):
    cm = pl.BlockSpec((batch, HY_CG, seq), lambda g: (0, g, 0))
    return pl.pallas_call(
        functools.partial(_hyena_kernel, batch=batch, seq=seq),
        grid=(D_HY // HY_CG,),
        in_specs=[pl.BlockSpec(memory_space=pltpu.SMEM), cm, cm, cm,
                  pl.BlockSpec((2, HY_CG, 2 * seq), lambda g: (0, g, 0))],
        out_specs=cm,
        out_shape=jax.ShapeDtypeStruct((batch, D_HY, seq), F32),
        scratch_shapes=[pltpu.VMEM((HY_T, 2 * seq), BF16), pltpu.VMEM((seq // HY_T * batch, HY_T), F32)],
        compiler_params=_params(("parallel",)),
        name="hyena_long_conv",
    )(bias_d, x1t, x2t, vt, kf)


def _outproj_kernel(attn_ref, hy_ref, cf_ref, x_ref, g1_ref, w_ref, ng_ref, sc_ref, sh_ref, rw_ref,
                    xo_ref, h2_ref, lt_ref):
    y = jnp.dot(attn_ref[...], w_ref[:D_NA], preferred_element_type=F32)
    y = y + jnp.dot(hy_ref[0].T.astype(BF16), w_ref[D_NA:D_NA + D_HY], preferred_element_type=F32)
    y = y + jnp.dot(cf_ref[...], w_ref[D_NA + D_HY:], preferred_element_type=F32)
    xn = x_ref[...] + g1_ref[0] * y
    xo_ref[...] = xn
    h2 = _rms_mod(xn, ng_ref[...], sc_ref[0], sh_ref[0]).astype(BF16)
    h2_ref[...] = h2
    lt_ref[0] = _nt_dot(rw_ref[...], h2)


def _out_projection(attn, hyt, cf, x2d, g1, w_bf, ng, sc, sh, rw_t, batch, seq, tm):
    n = x2d.shape[0]
    tps = seq // tm
    row = lambda w: pl.BlockSpec((tm, w), lambda i: (i, 0))
    full = lambda a: pl.BlockSpec(a.shape, lambda i: (0, 0))
    return pl.pallas_call(
        _outproj_kernel,
        grid=(n // tm,),
        in_specs=[row(D_NA), pl.BlockSpec((1, D_HY, tm), lambda i: (i // tps, 0, i % tps)), row(D_CF), row(D_MODEL),
                  _per_sample_spec(g1, tps), full(w_bf), full(ng), _per_sample_spec(sc, tps),
                  _per_sample_spec(sh, tps), full(rw_t)],
        out_specs=[row(D_MODEL), row(D_MODEL), pl.BlockSpec((1, N_EXPERTS, tm), lambda i: (i // tps, 0, i % tps))],
        out_shape=[jax.ShapeDtypeStruct((n, D_MODEL), F32), jax.ShapeDtypeStruct((n, D_MODEL), BF16),
                   jax.ShapeDtypeStruct((batch, N_EXPERTS, seq), F32)],
        compiler_params=_params(("parallel",)),
        name="out_projection",
    )(attn, hyt, cf, x2d, g1, w_bf, ng, sc, sh, rw_t)


def _route_kernel(lt_ref, tri_ref, pos_ref, aff_ref, *, cap):
    lg = lt_ref[...]
    bs, _, seq = lg.shape
    e = jnp.exp(lg - jnp.max(lg, axis=1, keepdims=True))
    aff = e / jnp.sum(e, axis=1, keepdims=True)
    aff_ref[...] = aff
    keys = pltpu.bitcast(aff.reshape(bs * N_EXPERTS, seq), I32)

    def body(i, thr):
        cand = thr | (jnp.int32(1) << (30 - i))
        cnt = jnp.sum(jnp.where(keys >= cand, 1.0, 0.0), axis=1, keepdims=True)
        return jnp.where(cnt >= cap, cand, thr)

    thr = lax.fori_loop(0, 31, body, jnp.zeros((bs * N_EXPERTS, 1), I32))
    gt = keys > thr
    eq = keys == thr
    need = cap - jnp.sum(jnp.where(gt, 1.0, 0.0), axis=1, keepdims=True)
    rank = jnp.dot(jnp.where(eq, 1.0, 0.0).astype(BF16), tri_ref[...], preferred_element_type=F32)
    sel = gt | (eq & (rank < need))
    slot = jnp.dot(jnp.where(sel, 1.0, 0.0).astype(BF16), tri_ref[...], preferred_element_type=F32)
    pos_ref[...] = jnp.where(sel, slot, -1.0).astype(I32).reshape(bs, N_EXPERTS, seq)


def _routing(logits_t, tri, cap):
    batch, _, seq = logits_t.shape
    bs = 8 if batch % 8 == 0 else batch
    blk = pl.BlockSpec((bs, N_EXPERTS, seq), lambda i: (i, 0, 0))
    return pl.pallas_call(
        functools.partial(_route_kernel, cap=cap),
        grid=(batch // bs,),
        in_specs=[blk, pl.BlockSpec((seq, seq), lambda i: (0, 0))],
        out_specs=[blk, blk],
        out_shape=[jax.ShapeDtypeStruct((batch, N_EXPERTS, seq), I32),
                   jax.ShapeDtypeStruct((batch, N_EXPERTS, seq), F32)],
        compiler_params=_params(("parallel",)),
        name="ec_routing",
    )(logits_t, tri)


def _gather_kernel(pos_ref, h_ref, xe_ref, *, cap):
    seq = h_ref.shape[0]
    slot = lax.broadcasted_iota(I32, (cap, seq), 0)
    for e in range(N_EXPERTS):
        onehot = jnp.where(pos_ref[0, e:e + 1, :] == slot, 1.0, 0.0).astype(BF16)
        xe_ref[0, e * cap:(e + 1) * cap, :] = jnp.dot(onehot, h_ref[...], preferred_element_type=F32).astype(BF16)


def _gather_tokens(pos, h2, cap):
    batch, _, seq = pos.shape
    return pl.pallas_call(
        functools.partial(_gather_kernel, cap=cap),
        grid=(batch,),
        in_specs=[pl.BlockSpec((1, N_EXPERTS, seq), lambda b: (b, 0, 0)),
                  pl.BlockSpec((seq, D_MODEL), lambda b: (b, 0))],
        out_specs=pl.BlockSpec((1, N_EXPERTS * cap, D_MODEL), lambda b: (b, 0, 0)),
        out_shape=jax.ShapeDtypeStruct((batch, N_EXPERTS * cap, D_MODEL), BF16),
        compiler_params=_params(("parallel",)),
        name="ec_gather",
    )(pos, h2)


FFN_ROWS = 1024
FFN_TF = 512


def _ffn_kernel(xe_ref, w1_ref, w3_ref, w2_ref, y_ref, acc_ref):
    f = pl.program_id(2)
    bt, cap, _ = xe_ref.shape
    x = xe_ref[...].reshape(bt * cap, D_MODEL)
    a = jnp.dot(x, w1_ref[0, 0].astype(BF16), preferred_element_type=F32)
    u = jnp.dot(x, w3_ref[0, 0].astype(BF16), preferred_element_type=F32)
    h = (a * jax.nn.sigmoid(a) * u).astype(BF16)
    part = jnp.dot(h, w2_ref[0, 0].astype(BF16), preferred_element_type=F32)

    @pl.when(f == 0)
    def _():
        acc_ref[...] = part

    @pl.when(f > 0)
    def _():
        acc_ref[...] += part

    @pl.when(f == pl.num_programs(2) - 1)
    def _():
        y_ref[...] = acc_ref[...].astype(BF16).reshape(bt, cap, D_MODEL)


def _expert_ffn(xe, w1, w3, w2, layer, cap):
    batch = xe.shape[0]
    bt = max(1, min(batch, FFN_ROWS // cap))
    while batch % bt:
        bt -= 1
    xspec = pl.BlockSpec((bt, cap, D_MODEL), lambda e, r, f: (r, e, 0))
    return pl.pallas_call(
        _ffn_kernel,
        grid=(N_EXPERTS, batch // bt, EXPERT_FF // FFN_TF),
        in_specs=[xspec, pl.BlockSpec((1, 1, D_MODEL, FFN_TF), lambda e, r, f: (layer, e, 0, f)),
                  pl.BlockSpec((1, 1, D_MODEL, FFN_TF), lambda e, r, f: (layer, e, 0, f)),
                  pl.BlockSpec((1, 1, FFN_TF, D_MODEL), lambda e, r, f: (layer, e, f, 0))],
        out_specs=xspec,
        out_shape=jax.ShapeDtypeStruct(xe.shape, BF16),
        scratch_shapes=[pltpu.VMEM((bt * cap, D_MODEL), F32)],
        compiler_params=_params(("parallel", "arbitrary", "arbitrary")),
        name="ec_expert_ffn",
    )(xe, w1, w3, w2)


def _scatter_kernel(pos_ref, aff_ref, y_ref, x_ref, g2_ref, fg_ref, o_ref, pg_scr, *, cap, final_norm):
    tl = x_ref.shape[0]
    slot = lax.broadcasted_iota(I32, (cap, tl), 0)
    for e in range(N_EXPERTS):
        hit = pos_ref[0, e:e + 1, :] == slot
        pg_scr[e * cap:(e + 1) * cap, :] = jnp.where(hit, aff_ref[0, e:e + 1, :], 0.0).astype(BF16)
    moe = _tn_dot(pg_scr[...], y_ref[0])
    xn = x_ref[...] + g2_ref[0] * moe
    if final_norm:
        xn = xn * lax.rsqrt(jnp.mean(xn * xn, axis=-1, keepdims=True) + EPS) * fg_ref[...]
    o_ref[...] = xn


def _scatter_residual(pos, aff, y, x2d, g2, fg, cap, tl, final_norm):
    batch, _, seq = pos.shape
    tps = seq // tl
    rspec = pl.BlockSpec((1, N_EXPERTS, tl), lambda b, t: (b, 0, t))
    xspec = pl.BlockSpec((tl, D_MODEL), lambda b, t: (b * tps + t, 0))
    g2spec = (pl.BlockSpec((1, 1, D_MODEL), lambda b, t: (0, 0, 0)) if g2.shape[0] == 1
              else pl.BlockSpec((1, 1, D_MODEL), lambda b, t: (b, 0, 0)))
    return pl.pallas_call(
        functools.partial(_scatter_kernel, cap=cap, final_norm=final_norm),
        grid=(batch, tps),
        in_specs=[rspec, rspec, pl.BlockSpec((1, N_EXPERTS * cap, D_MODEL), lambda b, t: (b, 0, 0)), xspec,
                  g2spec, pl.BlockSpec((1, D_MODEL), lambda b, t: (0, 0))],
        out_specs=xspec,
        out_shape=jax.ShapeDtypeStruct(x2d.shape, F32),
        scratch_shapes=[pltpu.VMEM((N_EXPERTS * cap, tl), BF16)],
        compiler_params=_params(("parallel", "arbitrary")),
        name="ec_scatter_residual",
    )(pos, aff, y, x2d, g2, fg)


def _strict_lower_ones(n):
    i = np.arange(n)
    return jnp.asarray(i[:, None] < i[None, :], dtype=BF16)


def _stream_layer(x2d, mods, lw, batch, seq, tm, attn_fn, final_norm):
    sh1, sc1, g1, sh2, sc2, g2 = mods
    qkv, hy_in, cf_in = _in_projection(x2d, lw["norm1_g"], sc1, sh1, lw["w_in"], seq, tm)
    attn = attn_fn(qkv)
    x1t, x2t, vt, cf = _conv_mixers(hy_in, cf_in, lw["hy_short_w"], lw["hy_short_b"], lw["cf_dw_w"], lw["cf_dw_b"],
                                    lw["cf_ln_g"], lw["cf_ln_b"], batch, seq)
    kf = _hyena_filters(seq, lw["hy_filt_w1"], lw["hy_filt_b1"], lw["hy_filt_freq"], lw["hy_filt_w2"],
                        lw["hy_filt_b2"], lw["hy_filt_w3"])
    hyt = _hyena_long_conv(x1t, x2t, vt, kf, lw["hy_bias_d"], batch, seq)
    x_mid, h2, logits_t = _out_projection(attn, hyt, cf, x2d, g1, lw["w_out"], lw["norm2_g"], sc2, sh2,
                                          lw["router_wt"], batch, seq, tm)
    cap = EC_CAPACITY * seq // N_EXPERTS
    pos, aff = _routing(logits_t, _strict_lower_ones(seq), cap)
    xe = _gather_tokens(pos, h2, cap)
    y = _expert_ffn(xe, lw["expert_w1"], lw["expert_w3"], lw["expert_w2"], lw["layer"], cap)
    return _scatter_residual(pos, aff, y, x_mid, g2, lw["final_norm_g"], cap, min(seq, 512), final_norm), qkv


def kernel(x, c, ctx, c_ctx, w_mod, b_mod, norm1_g, norm2_g, w_in, na_rpb, hy_short_w, hy_short_b, hy_filt_w1,
           hy_filt_b1, hy_filt_freq, hy_filt_w2, hy_filt_b2, hy_filt_w3, hy_bias_d, cf_dw_w, cf_dw_b, cf_ln_g,
           cf_ln_b, w_out, router_w, expert_w1, expert_w3, expert_w2, final_norm_g):
    batch, seq, _ = x.shape
    ctx_len = ctx.shape[1]
    depth = w_mod.shape[0]
    rows = seq // GRID_W

    n_c = batch + 1
    n_c_pad = -(-n_c // 8) * 8
    cvec = jnp.concatenate([c, c_ctx[None, :], jnp.zeros((n_c_pad - n_c, D_MODEL), F32)], axis=0)
    mod_all = _modulation(cvec, w_mod, b_mod)

    xl = x.reshape(batch * seq, D_MODEL)
    xc = ctx.reshape(batch * ctx_len, D_MODEL)
    for l in range(depth):
        last = l == depth - 1
        lw = dict(
            norm1_g=norm1_g[l][None, :], norm2_g=norm2_g[l][None, :], w_in=w_in[l].astype(BF16),
            hy_short_w=hy_short_w[l], hy_short_b=hy_short_b[l][None, :], hy_filt_w1=hy_filt_w1[l],
            hy_filt_b1=hy_filt_b1[l], hy_filt_freq=hy_filt_freq[l], hy_filt_w2=hy_filt_w2[l],
            hy_filt_b2=hy_filt_b2[l], hy_filt_w3=hy_filt_w3[l], hy_bias_d=hy_bias_d[l], cf_dw_w=cf_dw_w[l],
            cf_dw_b=cf_dw_b[l][None, :], cf_ln_g=cf_ln_g[l][None, :], cf_ln_b=cf_ln_b[l][None, :],
            w_out=w_out[l].astype(BF16), router_wt=router_w[l].T.astype(BF16),
            expert_w1=expert_w1, expert_w3=expert_w3, expert_w2=expert_w2, layer=l,
            final_norm_g=final_norm_g[None, :])
        chunks = [mod_all[l, :, j * D_MODEL:(j + 1) * D_MODEL] for j in range(N_MOD)]
        mods_l = [m[:batch, None, :] for m in chunks]
        mods_c = [m[batch:batch + 1, None, :] for m in chunks]
        bias = _na_bias_tiles(na_rpb[l], rows)

        if last:
            qkv_c, _, _ = _in_projection(xc, lw["norm1_g"], mods_c[1], mods_c[0], lw["w_in"], ctx_len, ctx_len)
        else:
            xc, qkv_c = _stream_layer(xc, mods_c, lw, batch, ctx_len, ctx_len,
                                      lambda qkv: _context_attention(qkv, batch, ctx_len), False)
        xl, _ = _stream_layer(xl, mods_l, lw, batch, seq, 512,
                              lambda qkv: _neighbourhood_attention(qkv, qkv_c, bias, batch, seq, ctx_len), last)
    return xl.reshape(batch, seq, D_MODEL)
```

```python
import functools
import math

import numpy as np
import jax
import jax.numpy as jnp
from jax import lax
from jax.experimental import pallas as pl
from jax.experimental.pallas import tpu as pltpu

F32, BF16, I32 = jnp.float32, jnp.bfloat16, jnp.int32
HIGHEST = lax.Precision.HIGHEST

D_MODEL = 1024
GRID_W = 64
N_MOD = 6
EPS = 1e-6
NEG_INF = -1e30
NA_HEAD_DIM = 64
D_NA = 512
NA_HEADS = 8
NA_WIN_H = 8
NA_WIN_W = 16
D_HY = 256
HY_EMB = 33
HY_FAST_DECAY = 0.3
HY_SLOW_DECAY = 1.5
HY_TARGET = 1e-2
D_CF = 256
CF_KERNEL = 31
D_QKV = 3 * D_NA
V_SLOT = 128
D_QKVX = 2 * D_NA + NA_HEADS * V_SLOT
Q_SCALE = NA_HEAD_DIM ** -0.5 * math.log2(math.e)
D_HYIN = 3 * D_HY
D_CFIN = 2 * D_CF
N_EXPERTS = 16
EC_CAPACITY = 2
EXPERT_FF = 2 * D_MODEL

NA_QROWS = 4
NA_KROWS = NA_QROWS + NA_WIN_H - 1
HY_T = 256
MIB = 1024 * 1024
VMEM_LIMIT = 56 * MIB


def _params(sem, vmem=VMEM_LIMIT):
    return pltpu.CompilerParams(dimension_semantics=sem, vmem_limit_bytes=vmem)


def _nt_dot(a, b):
    return lax.dot_general(a, b, (((1,), (1,)), ((), ())), preferred_element_type=F32)


def _tn_dot(a, b):
    return lax.dot_general(a, b, (((0,), (0,)), ((), ())), preferred_element_type=F32)


def _mod_kernel(c_ref, w_ref, b_ref, o_ref):
    c = c_ref[...]
    s = c * jax.nn.sigmoid(c)
    o_ref[0] = jnp.dot(s, w_ref[0], preferred_element_type=F32, precision=HIGHEST) + b_ref[0]


def _modulation(cvec, w_mod, b_mod):
    depth, _, n = w_mod.shape
    rows = cvec.shape[0]
    tn = 1536
    return pl.pallas_call(
        _mod_kernel,
        grid=(depth, n // tn),
        in_specs=[pl.BlockSpec((rows, D_MODEL), lambda l, j: (0, 0)),
                  pl.BlockSpec((1, D_MODEL, tn), lambda l, j: (l, 0, j)),
                  pl.BlockSpec((1, 1, tn), lambda l, j: (l, 0, j))],
        out_specs=pl.BlockSpec((1, rows, tn), lambda l, j: (l, 0, j)),
        out_shape=jax.ShapeDtypeStruct((depth, rows, n), F32),
        compiler_params=_params(("arbitrary", "arbitrary")),
        name="adaln_mod",
    )(cvec, w_mod, b_mod.reshape(depth, 1, n))


def _rms_mod(x, g, sc, sh):
    y = x * lax.rsqrt(jnp.mean(x * x, axis=-1, keepdims=True) + EPS)
    return (y * g) * (1.0 + sc) + sh


def _inproj_kernel(x_ref, g_ref, sc_ref, sh_ref, w_ref, qkv_ref, hy_ref, cf_ref):
    h = _rms_mod(x_ref[...], g_ref[...], sc_ref[0], sh_ref[0]).astype(BF16)
    qkv_ref[:, :D_NA] = (jnp.dot(h, w_ref[:, :D_NA], preferred_element_type=F32) * Q_SCALE).astype(BF16)
    qkv_ref[:, D_NA:2 * D_NA] = jnp.dot(h, w_ref[:, D_NA:2 * D_NA], preferred_element_type=F32).astype(BF16)
    v = jnp.dot(h, w_ref[:, 2 * D_NA:D_QKV], preferred_element_type=F32).astype(BF16)
    lane = lax.broadcasted_iota(I32, (v.shape[0], V_SLOT - NA_HEAD_DIM), 1)
    ones_col = jnp.where(lane == 0, 1.0, 0.0).astype(BF16)
    for hd in range(NA_HEADS):
        lo = 2 * D_NA + hd * V_SLOT
        qkv_ref[:, lo:lo + NA_HEAD_DIM] = v[:, hd * NA_HEAD_DIM:(hd + 1) * NA_HEAD_DIM]
        qkv_ref[:, lo + NA_HEAD_DIM:lo + V_SLOT] = ones_col
    hy_ref[...] = jnp.dot(h, w_ref[:, D_QKV:D_QKV + D_HYIN], preferred_element_type=F32)
    cf_ref[...] = jnp.dot(h, w_ref[:, D_QKV + D_HYIN:], preferred_element_type=F32)


def _per_sample_spec(arr, tiles_per_sample):
    if arr.shape[0] == 1:
        return pl.BlockSpec((1, 1, D_MODEL), lambda i: (0, 0, 0))
    return pl.BlockSpec((1, 1, D_MODEL), lambda i: (i // tiles_per_sample, 0, 0))


def _in_projection(x2d, g, sc, sh, w_bf, seq, tm):
    n = x2d.shape[0]
    tps = seq // tm
    d_in = w_bf.shape[1]
    row = lambda w: pl.BlockSpec((tm, w), lambda i: (i, 0))
    return pl.pallas_call(
        _inproj_kernel,
        grid=(n // tm,),
        in_specs=[row(D_MODEL), pl.BlockSpec((1, D_MODEL), lambda i: (0, 0)),
                  _per_sample_spec(sc, tps), _per_sample_spec(sh, tps),
                  pl.BlockSpec((D_MODEL, d_in), lambda i: (0, 0))],
        out_specs=[row(D_QKVX), row(D_HYIN), row(D_CFIN)],
        out_shape=[jax.ShapeDtypeStruct((n, D_QKVX), BF16), jax.ShapeDtypeStruct((n, D_HYIN), F32),
                   jax.ShapeDtypeStruct((n, D_CFIN), F32)],
        compiler_params=_params(("parallel",)),
        name="in_projection",
    )(x2d, g, sc, sh, w_bf)


def _na_bias_tiles(rpb, rows):
    col = np.arange(GRID_W)
    c0 = np.clip(col - NA_WIN_W // 2, 0, GRID_W - NA_WIN_W)
    col_ok = (col[None, :] >= c0[:, None]) & (col[None, :] < c0[:, None] + NA_WIN_W)
    dc = np.clip(col[None, :] - col[:, None], 1 - NA_WIN_W, NA_WIN_W - 1) + NA_WIN_W - 1
    n_groups = rows // NA_QROWS
    n_dr, n_dc = 2 * NA_WIN_H - 1, 2 * NA_WIN_W - 1
    dr_hot = np.zeros((3, NA_QROWS, NA_KROWS, n_dr), np.float32)
    ok = np.zeros((3, NA_QROWS, GRID_W, NA_KROWS, GRID_W), bool)
    for v, grp in enumerate((0, 1, n_groups - 1)):
        r = grp * NA_QROWS + np.arange(NA_QROWS)
        k0 = np.clip(grp * NA_QROWS - NA_WIN_H // 2, 0, rows - NA_KROWS)
        kr = k0 + np.arange(NA_KROWS)
        r0 = np.clip(r - NA_WIN_H // 2, 0, rows - NA_WIN_H)
        row_ok = (kr[None, :] >= r0[:, None]) & (kr[None, :] < r0[:, None] + NA_WIN_H)
        dr = kr[None, :] - r[:, None] + NA_WIN_H - 1
        for i, j in zip(*np.nonzero(row_ok)):
            dr_hot[v, i, j, dr[i, j]] = 1.0
        ok[v] = row_ok[:, None, :, None] & col_ok[None, :, None, :]
    dc_hot = (dc[None, :, :] == np.arange(n_dc)[:, None, None]).astype(np.float32)
    t1 = jnp.einsum('hrd,dqk->hrqk', rpb.astype(F32), jnp.asarray(dc_hot), precision=HIGHEST)
    bias = jnp.einsum('vijr,hrqk->vhiqjk', jnp.asarray(dr_hot), t1, precision=HIGHEST)
    bias = jnp.where(jnp.asarray(ok)[:, None], bias * math.log2(math.e), NEG_INF)
    return bias.reshape(3, NA_HEADS, NA_QROWS * GRID_W, NA_KROWS * GRID_W)


def _softmax_pv(s_list, v_list):
    m = functools.reduce(jnp.maximum, [jnp.max(s, axis=-1, keepdims=True) for s in s_list])
    o = functools.reduce(jnp.add, [jnp.dot(jnp.exp2(s - m).astype(BF16), v, preferred_element_type=F32)
                                   for s, v in zip(s_list, v_list)])
    return o[:, :NA_HEAD_DIM] / o[:, NA_HEAD_DIM:NA_HEAD_DIM + 1]


def _na_kernel(q_ref, k_ref, v_ref, kc_ref, vc_ref, bias_ref, o_ref, *, rows):
    g = pl.program_id(1)
    k0 = jnp.clip(g * NA_QROWS - NA_WIN_H // 2, 0, rows - NA_KROWS) * GRID_W
    k0 = pl.multiple_of(k0, GRID_W)
    nk = NA_KROWS * GRID_W
    def scores(h):
        sl = slice(h * NA_HEAD_DIM, (h + 1) * NA_HEAD_DIM)
        qh = q_ref[:, sl]
        return [_nt_dot(qh, k_ref[pl.ds(k0, nk), sl]) + bias_ref[0, h], _nt_dot(qh, kc_ref[:, sl])]

    s = scores(0)
    for h in range(NA_HEADS):
        s_next = scores(h + 1) if h + 1 < NA_HEADS else None
        vs = slice(h * V_SLOT, (h + 1) * V_SLOT)
        o = _softmax_pv(s, [v_ref[pl.ds(k0, nk), vs], vc_ref[:, vs]])
        o_ref[:, h * NA_HEAD_DIM:(h + 1) * NA_HEAD_DIM] = o.astype(o_ref.dtype)
        s = s_next


def _neighbourhood_attention(qkv, qkv_c, bias, batch, seq, ctx_len):
    rows = seq // GRID_W
    n_groups = rows // NA_QROWS
    nq, nk = NA_QROWS * GRID_W, NA_KROWS * GRID_W
    sel = lambda g: jnp.where(g == 0, 0, jnp.where(g == n_groups - 1, 2, 1))
    return pl.pallas_call(
        functools.partial(_na_kernel, rows=rows),
        grid=(batch, n_groups),
        in_specs=[pl.BlockSpec((nq, D_NA), lambda b, g: (b * n_groups + g, 0)),
                  pl.BlockSpec((seq, D_NA), lambda b, g: (b, 1)),
                  pl.BlockSpec((seq, NA_HEADS * V_SLOT), lambda b, g: (b, 1)),
                  pl.BlockSpec((ctx_len, D_NA), lambda b, g: (b, 1)),
                  pl.BlockSpec((ctx_len, NA_HEADS * V_SLOT), lambda b, g: (b, 1)),
                  pl.BlockSpec((1, NA_HEADS, nq, nk), lambda b, g: (sel(g), 0, 0, 0))],
        out_specs=pl.BlockSpec((nq, D_NA), lambda b, g: (b * n_groups + g, 0)),
        out_shape=jax.ShapeDtypeStruct((batch * seq, D_NA), BF16),
        compiler_params=_params(("parallel", "arbitrary")),
        name="neighbourhood_attention",
    )(qkv, qkv, qkv, qkv_c, qkv_c, bias)


def _ctx_attn_kernel(q_ref, k_ref, v_ref, o_ref):
    for h in range(NA_HEADS):
        sl = slice(h * NA_HEAD_DIM, (h + 1) * NA_HEAD_DIM)
        o = _softmax_pv([_nt_dot(q_ref[:, sl], k_ref[:, sl])], [v_ref[:, h * V_SLOT:(h + 1) * V_SLOT]])
        o_ref[:, sl] = o.astype(o_ref.dtype)


def _context_attention(qkv_c, batch, ctx_len):
    spec = lambda j: pl.BlockSpec((ctx_len, D_NA), lambda b: (b, j))
    return pl.pallas_call(
        _ctx_attn_kernel,
        grid=(batch,),
        in_specs=[spec(0), spec(1), pl.BlockSpec((ctx_len, NA_HEADS * V_SLOT), lambda b: (b, 1))],
        out_specs=spec(0),
        out_shape=jax.ShapeDtypeStruct((batch * ctx_len, D_NA), BF16),
        compiler_params=_params(("parallel",)),
        name="context_attention",
    )(qkv_c, qkv_c, qkv_c)


SUBLANES = 8
LANES = 128
SHORT_CHUNK = 128
DW_CHUNK = 128
DW_STAGES = 2
HY_HALO = 8
CF_HALO = 16


def _conv_kernel(hy_ref, cf_ref, sw_ref, sb_ref, dw_ref, db_ref, lg_ref, lb_ref,
                 x1_ref, x2_ref, v_ref, cfo_ref, hpad, gpad, ph_scr, *, seq):
    hpad[0:HY_HALO, :] = jnp.zeros((HY_HALO, D_HYIN), F32)
    hpad[seq + HY_HALO:seq + 2 * HY_HALO, :] = jnp.zeros((HY_HALO, D_HYIN), F32)
    hpad[HY_HALO:seq + HY_HALO, :] = hy_ref[...]
    gpad[0:CF_HALO, :] = jnp.zeros((CF_HALO, D_CF), F32)
    gpad[seq + CF_HALO:seq + 2 * CF_HALO, :] = jnp.zeros((CF_HALO, D_CF), F32)
    gpad[CF_HALO:seq + CF_HALO, :] = cf_ref[:, :D_CF] * jax.nn.sigmoid(cf_ref[:, D_CF:])
    pad_s = 1
    pad_c = (CF_KERNEL - 1) // 2

    def short_body(i, carry):
        ch = SHORT_CHUNK
        base = pl.multiple_of(i * ch, ch)
        win = hpad[pl.ds(base, ch + 2 * HY_HALO), :]
        u = sb_ref[...]
        for t in range(3):
            o = HY_HALO - pad_s + t
            u = u + win[o:o + ch, :] * sw_ref[t:t + 1, :]
        x1_ref[0, :, pl.ds(base, ch)] = u[:, :D_HY].T
        x2_ref[0, :, pl.ds(base, ch)] = u[:, D_HY:2 * D_HY].T
        v_ref[0, :, pl.ds(base, ch)] = u[:, 2 * D_HY:].T
        return carry

    lax.fori_loop(0, seq // SHORT_CHUNK, short_body, 0)

    def dw_chunk(base, stage):
        ch = DW_CHUNK
        gw = gpad[pl.ds(base, ch + 2 * CF_HALO), :]
        span = ch + 2 * CF_HALO - SUBLANES
        for phase in range(1, SUBLANES):
            ph_scr[stage, phase - 1] = gw[phase:phase + span, :]
        y = db_ref[...]
        for t in range(CF_KERNEL):
            o = CF_HALO - pad_c + t
            phase = o % SUBLANES
            if phase == 0:
                src = gpad[pl.ds(base + o, ch), :]
            else:
                src = ph_scr[stage, phase - 1, o - phase:o - phase + ch, :]
            y = y + src * dw_ref[t:t + 1, :]
        mu = jnp.mean(y, axis=-1, keepdims=True)
        yc = y - mu
        var = jnp.mean(yc * yc, axis=-1, keepdims=True)
        yn = yc * lax.rsqrt(var + EPS) * lg_ref[...] + lb_ref[...]
        cfo_ref[pl.ds(base, ch), :] = (yn * jax.nn.sigmoid(yn)).astype(cfo_ref.dtype)

    def dw_body(i, carry):
        for stage in range(DW_STAGES):
            dw_chunk(pl.multiple_of((i * DW_STAGES + stage) * DW_CHUNK, DW_CHUNK), stage)
        return carry

    lax.fori_loop(0, seq // (DW_CHUNK * DW_STAGES), dw_body, 0)


def _conv_mixers(hy_in, cf_in, sw, sb, dw, db, lg, lb, batch, seq):
    full = lambda a: pl.BlockSpec(a.shape, lambda b: (0, 0))
    cm = pl.BlockSpec((1, D_HY, seq), lambda b: (b, 0, 0))
    cm_shape = jax.ShapeDtypeStruct((batch, D_HY, seq), F32)
    return pl.pallas_call(
        functools.partial(_conv_kernel, seq=seq),
        grid=(batch,),
        in_specs=[pl.BlockSpec((seq, D_HYIN), lambda b: (b, 0)), pl.BlockSpec((seq, D_CFIN), lambda b: (b, 0)),
                  full(sw), full(sb), full(dw), full(db), full(lg), full(lb)],
        out_specs=[cm, cm, cm, pl.BlockSpec((seq, D_CF), lambda b: (b, 0))],
        out_shape=[cm_shape, cm_shape, cm_shape, jax.ShapeDtypeStruct((batch * seq, D_CF), BF16)],
        scratch_shapes=[pltpu.VMEM((seq + 2 * HY_HALO, D_HYIN), F32), pltpu.VMEM((seq + 2 * CF_HALO, D_CF), F32),
                        pltpu.VMEM((DW_STAGES, SUBLANES - 1, DW_CHUNK + 2 * CF_HALO - SUBLANES, D_CF), F32)],
        compiler_params=_params(("parallel",)),
        name="conv_mixers",
    )(hy_in, cf_in, sw, sb, dw, db, lg, lb)


def _filter_kernel(fb_ref, w1t_ref, w1c_ref, w1s_ref, b1_ref, fr_ref, w2_ref, b2_ref, w3_ref, rate_ref, o_ref, *, seq):
    pos = lax.broadcasted_iota(I32, (seq, 1), 0).astype(F32)
    t = pos / max(seq - 1, 1)
    ang = fb_ref[...] * (2.0 * math.pi) * pos / seq
    dot = functools.partial(jnp.dot, preferred_element_type=F32, precision=HIGHEST)
    fr = fr_ref[...]
    z1 = t * w1t_ref[...] + dot(jnp.cos(ang), w1c_ref[...]) + dot(-jnp.sin(ang), w1s_ref[...]) + b1_ref[...]
    h = jnp.sin(fr * z1)
    h = jnp.sin(fr * (dot(h, w2_ref[...]) + b2_ref[...]))
    o_ref[...] = dot(h, w3_ref[...]) * jnp.exp(-t * rate_ref[...])


def _hyena_filters(seq, w1, b1, freq, w2, b2, w3):
    bands = (HY_EMB - 1) // 2
    fb = jnp.linspace(1e-4, bands - 1, bands, dtype=F32)[None, :]
    min_decay = math.log(HY_TARGET) / HY_SLOW_DECAY
    max_decay = math.log(HY_TARGET) / HY_FAST_DECAY
    rate = jnp.abs(jnp.linspace(min_decay, max_decay, D_HY, dtype=F32))
    rate4 = jnp.tile(rate, 4)[None, :]
    args = (fb, w1[0:1], w1[1:1 + bands], w1[1 + bands:], b1[None, :], freq[None, :], w2, b2[None, :], w3, rate4)
    h = pl.pallas_call(
        functools.partial(_filter_kernel, seq=seq),
        out_shape=jax.ShapeDtypeStruct((seq, 4 * D_HY), F32),
        compiler_params=_params(None),
        name="hyena_filter",
    )(*args)
    h = h.reshape(seq, 2, 2, D_HY)
    fwd, bwd = h[:, 0], h[:, 1]
    line = jnp.concatenate([jnp.zeros_like(fwd[:1]), bwd[1:][::-1], fwd], axis=0)
    return jnp.transpose(line, (1, 2, 0))


HY_CG = 8


def _hyena_kernel(d_ref, x1_ref, x2_ref, v_ref, kf_ref, o_ref, w_scr, acc_scr, *, batch, seq):
    nb = seq // HY_T
    cg = pl.program_id(0)

    def blocks(ref, c):
        return jnp.concatenate([ref[:, c, j * HY_T:(j + 1) * HY_T] for j in range(nb)], axis=0)

    for c in range(HY_CG):
        z = blocks(v_ref, c)
        for n, gate_ref in enumerate((x1_ref, x2_ref)):
            line = jnp.broadcast_to(kf_ref[n, c:c + 1, :], (HY_T, 2 * seq))
            w_scr[...] = pltpu.roll(line, 0, 1, stride=1, stride_axis=0).astype(BF16)
            zb = z.astype(BF16)
            acc_scr[...] = jnp.dot(zb, w_scr[:, seq:seq + HY_T], preferred_element_type=F32)
            for d in range(1, nb):
                m = batch * (nb - d)
                acc_scr[batch * d:, :] += jnp.dot(zb[:m], w_scr[:, seq + d * HY_T:seq + (d + 1) * HY_T],
                                                  preferred_element_type=F32)
                acc_scr[:m, :] += jnp.dot(zb[batch * d:], w_scr[:, seq - d * HY_T:seq - (d - 1) * HY_T],
                                          preferred_element_type=F32)
            z = blocks(gate_ref, c) * (acc_scr[...] + d_ref[n, cg * HY_CG + c] * z)
        for j in range(nb):
            o_ref[:, c, j * HY_T:(j + 1) * HY_T] = z[j * batch:(j + 1) * batch]


def _hyena_long_conv(x1t, x2t, vt, kf, bias_d, batch, seq):
    cm = pl.BlockSpec((batch, HY_CG, seq), lambda g: (0, g, 0))
    return pl.pallas_call(
        functools.partial(_hyena_kernel, batch=batch, seq=seq),
        grid=(D_HY // HY_CG,),
        in_specs=[pl.BlockSpec(memory_space=pltpu.SMEM), cm, cm, cm,
                  pl.BlockSpec((2, HY_CG, 2 * seq), lambda g: (0, g, 0))],
        out_specs=cm,
        out_shape=jax.ShapeDtypeStruct((batch, D_HY, seq), F32),
        scratch_shapes=[pltpu.VMEM((HY_T, 2 * seq), BF16), pltpu.VMEM((seq // HY_T * batch, HY_T), F32)],
        compiler_params=_params(("parallel",)),
        name="hyena_long_conv",
    )(bias_d, x1t, x2t, vt, kf)


def _outproj_kernel(attn_ref, hy_ref, cf_ref, x_ref, g1_ref, w_ref, ng_ref, sc_ref, sh_ref, rw_ref,
                    xo_ref, h2_ref, lt_ref):
    y = jnp.dot(attn_ref[...], w_ref[:D_NA], preferred_element_type=F32)
    y = y + jnp.dot(hy_ref[0].T.astype(BF16), w_ref[D_NA:D_NA + D_HY], preferred_element_type=F32)
    y = y + jnp.dot(cf_ref[...], w_ref[D_NA + D_HY:], preferred_element_type=F32)
    xn = x_ref[...] + g1_ref[0] * y
    xo_ref[...] = xn
    h2 = _rms_mod(xn, ng_ref[...], sc_ref[0], sh_ref[0]).astype(BF16)
    h2_ref[...] = h2
    lt_ref[0] = _nt_dot(rw_ref[...], h2)


def _out_projection(attn, hyt, cf, x2d, g1, w_bf, ng, sc, sh, rw_t, batch, seq, tm):
    n = x2d.shape[0]
    tps = seq // tm
    row = lambda w: pl.BlockSpec((tm, w), lambda i: (i, 0))
    full = lambda a: pl.BlockSpec(a.shape, lambda i: (0, 0))
    return pl.pallas_call(
        _outproj_kernel,
        grid=(n // tm,),
        in_specs=[row(D_NA), pl.BlockSpec((1, D_HY, tm), lambda i: (i // tps, 0, i % tps)), row(D_CF), row(D_MODEL),
                  _per_sample_spec(g1, tps), full(w_bf), full(ng), _per_sample_spec(sc, tps),
                  _per_sample_spec(sh, tps), full(rw_t)],
        out_specs=[row(D_MODEL), row(D_MODEL), pl.BlockSpec((1, N_EXPERTS, tm), lambda i: (i // tps, 0, i % tps))],
        out_shape=[jax.ShapeDtypeStruct((n, D_MODEL), F32), jax.ShapeDtypeStruct((n, D_MODEL), BF16),
                   jax.ShapeDtypeStruct((batch, N_EXPERTS, seq), F32)],
        compiler_params=_params(("parallel",)),
        name="out_projection",
    )(attn, hyt, cf, x2d, g1, w_bf, ng, sc, sh, rw_t)


def _route_kernel(lt_ref, tri_ref, pos_ref, aff_ref, *, cap):
    lg = lt_ref[...]
    bs, _, seq = lg.shape
    e = jnp.exp(lg - jnp.max(lg, axis=1, keepdims=True))
    aff = e / jnp.sum(e, axis=1, keepdims=True)
    aff_ref[...] = aff
    keys = pltpu.bitcast(aff.reshape(bs * N_EXPERTS, seq), I32)

    def body(i, thr):
        cand = thr | (jnp.int32(1) << (30 - i))
        cnt = jnp.sum(jnp.where(keys >= cand, 1.0, 0.0), axis=1, keepdims=True)
        return jnp.where(cnt >= cap, cand, thr)

    thr = lax.fori_loop(0, 31, body, jnp.zeros((bs * N_EXPERTS, 1), I32))
    gt = keys > thr
    eq = keys == thr
    need = cap - jnp.sum(jnp.where(gt, 1.0, 0.0), axis=1, keepdims=True)
    rank = jnp.dot(jnp.where(eq, 1.0, 0.0).astype(BF16), tri_ref[...], preferred_element_type=F32)
    sel = gt | (eq & (rank < need))
    slot = jnp.dot(jnp.where(sel, 1.0, 0.0).astype(BF16), tri_ref[...], preferred_element_type=F32)
    pos_ref[...] = jnp.where(sel, slot, -1.0).astype(I32).reshape(bs, N_EXPERTS, seq)


def _routing(logits_t, tri, cap):
    batch, _, seq = logits_t.shape
    bs = 8 if batch % 8 == 0 else batch
    blk = pl.BlockSpec((bs, N_EXPERTS, seq), lambda i: (i, 0, 0))
    return pl.pallas_call(
        functools.partial(_route_kernel, cap=cap),
        grid=(batch // bs,),
        in_specs=[blk, pl.BlockSpec((seq, seq), lambda i: (0, 0))],
        out_specs=[blk, blk],
        out_shape=[jax.ShapeDtypeStruct((batch, N_EXPERTS, seq), I32),
                   jax.ShapeDtypeStruct((batch, N_EXPERTS, seq), F32)],
        compiler_params=_params(("parallel",)),
        name="ec_routing",
    )(logits_t, tri)


def _gather_kernel(pos_ref, h_ref, xe_ref, *, cap):
    seq = h_ref.shape[0]
    slot = lax.broadcasted_iota(I32, (cap, seq), 0)
    for e in range(N_EXPERTS):
        onehot = jnp.where(pos_ref[0, e:e + 1, :] == slot, 1.0, 0.0).astype(BF16)
        xe_ref[0, e * cap:(e + 1) * cap, :] = jnp.dot(onehot, h_ref[...], preferred_element_type=F32).astype(BF16)


def _gather_tokens(pos, h2, cap):
    batch, _, seq = pos.shape
    return pl.pallas_call(
        functools.partial(_gather_kernel, cap=cap),
        grid=(batch,),
        in_specs=[pl.BlockSpec((1, N_EXPERTS, seq), lambda b: (b, 0, 0)),
                  pl.BlockSpec((seq, D_MODEL), lambda b: (b, 0))],
        out_specs=pl.BlockSpec((1, N_EXPERTS * cap, D_MODEL), lambda b: (b, 0, 0)),
        out_shape=jax.ShapeDtypeStruct((batch, N_EXPERTS * cap, D_MODEL), BF16),
        compiler_params=_params(("parallel",)),
        name="ec_gather",
    )(pos, h2)


FFN_ROWS = 512


def _ffn_kernel(*refs, layer, n_ctx_tiles):
    if n_ctx_tiles:
        xe_ref, xc_ref, w1_hbm, w3_hbm, w2_hbm, y_ref, yc_ref, w1b, w3b, w2b, st1, st3, st2, sem = refs
    else:
        xe_ref, w1_hbm, w3_hbm, w2_hbm, y_ref, w1b, w3b, w2b, st1, st3, st2, sem = refs
    e, rt = pl.program_id(0), pl.program_id(1)
    n_e, n_rt = N_EXPERTS, pl.num_programs(1)
    r1, r2 = st1.shape[1], st2.shape[1]

    def slab_copies(expert, k, slot):
        o1, o2 = pl.multiple_of(k * r1, r1), pl.multiple_of(k * r2, r2)
        return (pltpu.make_async_copy(w1_hbm.at[layer, expert, pl.ds(o1, r1), :], st1.at[slot], sem.at[slot, 0]),
                pltpu.make_async_copy(w3_hbm.at[layer, expert, pl.ds(o1, r1), :], st3.at[slot], sem.at[slot, 1]),
                pltpu.make_async_copy(w2_hbm.at[layer, expert, pl.ds(o2, r2), :], st2.at[slot], sem.at[slot, 2]))

    def start_slab(expert, k, slot):
        for cp in slab_copies(expert, k, slot):
            cp.start()

    def land_slab(expert, k, slot, wslot):
        for cp in slab_copies(expert, k, slot):
            cp.wait()
        o1, o2 = pl.multiple_of(k * r1, r1), pl.multiple_of(k * r2, r2)
        w1b[wslot, pl.ds(o1, r1), :] = st1[slot].astype(BF16)
        w3b[wslot, pl.ds(o1, r1), :] = st3[slot].astype(BF16)
        w2b[wslot, pl.ds(o2, r2), :] = st2[slot].astype(BF16)

    n_slabs = D_MODEL // r1

    @pl.when((e == 0) & (rt == 0))
    def _():
        start_slab(0, 0, 0)
        for k in range(n_slabs):
            if k + 1 < n_slabs:
                start_slab(0, k + 1, (k + 1) % 2)
            land_slab(0, k, k % 2, 0)

        start_slab(1, 0, 0)

    @pl.when(e + 1 < n_e)
    def _():
        land_slab(e + 1, rt, rt % 2, (e + 1) % 2)

    last_tile = rt + 1 == n_rt
    nxt_e = jnp.where(last_tile, e + 2, e + 1)
    nxt_k = jnp.where(last_tile, 0, rt + 1)

    @pl.when(nxt_e < n_e)
    def _():
        start_slab(nxt_e, nxt_k, nxt_k % 2)

    ws = e % 2

    def swiglu(x_ref, o_ref):
        bt, cap, _ = x_ref.shape
        x = x_ref[...].reshape(bt * cap, D_MODEL)
        a = jnp.dot(x, w1b[ws], preferred_element_type=F32)
        u = jnp.dot(x, w3b[ws], preferred_element_type=F32)
        h = (a * jax.nn.sigmoid(a) * u).astype(BF16)
        o_ref[...] = jnp.dot(h, w2b[ws], preferred_element_type=F32).astype(BF16).reshape(bt, cap, D_MODEL)

    swiglu(xe_ref, y_ref)
    if n_ctx_tiles:
        @pl.when(rt < n_ctx_tiles)
        def _():
            swiglu(xc_ref, yc_ref)


def _tile_samples(batch, cap):
    bt = max(1, min(batch, FFN_ROWS // cap))
    while batch % bt:
        bt -= 1
    return bt


def _expert_ffn(xe, xe_ctx, w1, w3, w2, layer):
    batch = xe.shape[0]
    cap = xe.shape[1] // N_EXPERTS
    bt = _tile_samples(batch, cap)
    n_rt = batch // bt
    xspec = pl.BlockSpec((bt, cap, D_MODEL), lambda e, r: (r, e, 0))
    hbm = pl.BlockSpec(memory_space=pl.ANY)
    in_specs, out_specs, out_shape, args, n_ctx_tiles = [xspec], [xspec], [jax.ShapeDtypeStruct(xe.shape, BF16)], [xe], 0
    if xe_ctx is not None:
        cap_c = xe_ctx.shape[1] // N_EXPERTS
        bt_c = _tile_samples(batch, cap_c)
        n_ctx_tiles = batch // bt_c
        assert n_ctx_tiles <= n_rt
        cspec = pl.BlockSpec((bt_c, cap_c, D_MODEL), lambda e, r: (jnp.minimum(r, n_ctx_tiles - 1), e, 0))
        in_specs.append(cspec)
        out_specs.append(cspec)
        out_shape.append(jax.ShapeDtypeStruct(xe_ctx.shape, BF16))
        args.append(xe_ctx)
    assert D_MODEL % n_rt == 0 and EXPERT_FF % n_rt == 0
    r1, r2 = D_MODEL // n_rt, EXPERT_FF // n_rt
    out = pl.pallas_call(
        functools.partial(_ffn_kernel, layer=layer, n_ctx_tiles=n_ctx_tiles),
        grid=(N_EXPERTS, n_rt),
        in_specs=in_specs + [hbm, hbm, hbm],
        out_specs=out_specs,
        out_shape=out_shape,
        scratch_shapes=[pltpu.VMEM((2, D_MODEL, EXPERT_FF), BF16), pltpu.VMEM((2, D_MODEL, EXPERT_FF), BF16),
                        pltpu.VMEM((2, EXPERT_FF, D_MODEL), BF16), pltpu.VMEM((2, r1, EXPERT_FF), F32),
                        pltpu.VMEM((2, r1, EXPERT_FF), F32), pltpu.VMEM((2, r2, D_MODEL), F32),
                        pltpu.SemaphoreType.DMA((2, 3))],
        compiler_params=_params(("arbitrary", "arbitrary")),
        name="ec_expert_ffn",
    )(*args, w1, w3, w2)
    return (out[0], out[1]) if xe_ctx is not None else (out[0], None)


def _scatter_kernel(pos_ref, aff_ref, y_ref, x_ref, g2_ref, fg_ref, o_ref, pg_scr, *, cap, final_norm):
    tl = x_ref.shape[0]
    slot = lax.broadcasted_iota(I32, (cap, tl), 0)
    for e in range(N_EXPERTS):
        hit = pos_ref[0, e:e + 1, :] == slot
        pg_scr[e * cap:(e + 1) * cap, :] = jnp.where(hit, aff_ref[0, e:e + 1, :], 0.0).astype(BF16)
    moe = _tn_dot(pg_scr[...], y_ref[0])
    xn = x_ref[...] + g2_ref[0] * moe
    if final_norm:
        xn = xn * lax.rsqrt(jnp.mean(xn * xn, axis=-1, keepdims=True) + EPS) * fg_ref[...]
    o_ref[...] = xn


def _scatter_residual(pos, aff, y, x2d, g2, fg, cap, tl, final_norm):
    batch, _, seq = pos.shape
    tps = seq // tl
    rspec = pl.BlockSpec((1, N_EXPERTS, tl), lambda b, t: (b, 0, t))
    xspec = pl.BlockSpec((tl, D_MODEL), lambda b, t: (b * tps + t, 0))
    g2spec = (pl.BlockSpec((1, 1, D_MODEL), lambda b, t: (0, 0, 0)) if g2.shape[0] == 1
              else pl.BlockSpec((1, 1, D_MODEL), lambda b, t: (b, 0, 0)))
    return pl.pallas_call(
        functools.partial(_scatter_kernel, cap=cap, final_norm=final_norm),
        grid=(batch, tps),
        in_specs=[rspec, rspec, pl.BlockSpec((1, N_EXPERTS * cap, D_MODEL), lambda b, t: (b, 0, 0)), xspec,
                  g2spec, pl.BlockSpec((1, D_MODEL), lambda b, t: (0, 0))],
        out_specs=xspec,
        out_shape=jax.ShapeDtypeStruct(x2d.shape, F32),
        scratch_shapes=[pltpu.VMEM((N_EXPERTS * cap, tl), BF16)],
        compiler_params=_params(("parallel", "arbitrary")),
        name="ec_scatter_residual",
    )(pos, aff, y, x2d, g2, fg)


def _strict_lower_ones(n):
    i = np.arange(n)
    return jnp.asarray(i[:, None] < i[None, :], dtype=BF16)


def _stream_front(x2d, mods, lw, batch, seq, tm, attn_fn):
    sh1, sc1, g1, sh2, sc2, g2 = mods
    qkv, hy_in, cf_in = _in_projection(x2d, lw["norm1_g"], sc1, sh1, lw["w_in"], seq, tm)
    attn = attn_fn(qkv)
    x1t, x2t, vt, cf = _conv_mixers(hy_in, cf_in, lw["hy_short_w"], lw["hy_short_b"], lw["cf_dw_w"], lw["cf_dw_b"],
                                    lw["cf_ln_g"], lw["cf_ln_b"], batch, seq)
    kf = _hyena_filters(seq, lw["hy_filt_w1"], lw["hy_filt_b1"], lw["hy_filt_freq"], lw["hy_filt_w2"],
                        lw["hy_filt_b2"], lw["hy_filt_w3"])
    hyt = _hyena_long_conv(x1t, x2t, vt, kf, lw["hy_bias_d"], batch, seq)
    x_mid, h2, logits_t = _out_projection(attn, hyt, cf, x2d, g1, lw["w_out"], lw["norm2_g"], sc2, sh2,
                                          lw["router_wt"], batch, seq, tm)
    cap = EC_CAPACITY * seq // N_EXPERTS
    pos, aff = _routing(logits_t, _strict_lower_ones(seq), cap)
    xe = _gather_tokens(pos, h2, cap)
    return dict(pos=pos, aff=aff, xe=xe, x_mid=x_mid, g2=g2, cap=cap, seq=seq, qkv=qkv)


def _stream_back(front, y, lw, final_norm):
    return _scatter_residual(front["pos"], front["aff"], y, front["x_mid"], front["g2"], lw["final_norm_g"],
                             front["cap"], min(front["seq"], 512), final_norm)


def kernel(x, c, ctx, c_ctx, w_mod, b_mod, norm1_g, norm2_g, w_in, na_rpb, hy_short_w, hy_short_b, hy_filt_w1,
           hy_filt_b1, hy_filt_freq, hy_filt_w2, hy_filt_b2, hy_filt_w3, hy_bias_d, cf_dw_w, cf_dw_b, cf_ln_g,
           cf_ln_b, w_out, router_w, expert_w1, expert_w3, expert_w2, final_norm_g):
    batch, seq, _ = x.shape
    ctx_len = ctx.shape[1]
    depth = w_mod.shape[0]
    rows = seq // GRID_W

    n_c = batch + 1
    n_c_pad = -(-n_c // 8) * 8
    cvec = jnp.concatenate([c, c_ctx[None, :], jnp.zeros((n_c_pad - n_c, D_MODEL), F32)], axis=0)
    mod_all = _modulation(cvec, w_mod, b_mod)

    xl = x.reshape(batch * seq, D_MODEL)
    xc = ctx.reshape(batch * ctx_len, D_MODEL)
    for l in range(depth):
        last = l == depth - 1
        lw = dict(
            norm1_g=norm1_g[l][None, :], norm2_g=norm2_g[l][None, :], w_in=w_in[l].astype(BF16),
            hy_short_w=hy_short_w[l], hy_short_b=hy_short_b[l][None, :], hy_filt_w1=hy_filt_w1[l],
            hy_filt_b1=hy_filt_b1[l], hy_filt_freq=hy_filt_freq[l], hy_filt_w2=hy_filt_w2[l],
            hy_filt_b2=hy_filt_b2[l], hy_filt_w3=hy_filt_w3[l], hy_bias_d=hy_bias_d[l], cf_dw_w=cf_dw_w[l],
            cf_dw_b=cf_dw_b[l][None, :], cf_ln_g=cf_ln_g[l][None, :], cf_ln_b=cf_ln_b[l][None, :],
            w_out=w_out[l].astype(BF16), router_wt=router_w[l].T.astype(BF16),
            expert_w1=expert_w1, expert_w3=expert_w3, expert_w2=expert_w2, layer=l,
            final_norm_g=final_norm_g[None, :])
        chunks = [mod_all[l, :, j * D_MODEL:(j + 1) * D_MODEL] for j in range(N_MOD)]
        mods_l = [m[:batch, None, :] for m in chunks]
        mods_c = [m[batch:batch + 1, None, :] for m in chunks]
        bias = _na_bias_tiles(na_rpb[l], rows)

        if last:
            front_c = None
            qkv_c, _, _ = _in_projection(xc, lw["norm1_g"], mods_c[1], mods_c[0], lw["w_in"], ctx_len, ctx_len)
        else:
            front_c = _stream_front(xc, mods_c, lw, batch, ctx_len, ctx_len,
                                    lambda qkv: _context_attention(qkv, batch, ctx_len))
            qkv_c = front_c["qkv"]
        front_l = _stream_front(xl, mods_l, lw, batch, seq, 512,
                                lambda qkv: _neighbourhood_attention(qkv, qkv_c, bias, batch, seq, ctx_len))
        y_l, y_c = _expert_ffn(front_l["xe"], None if last else front_c["xe"], expert_w1, expert_w3, expert_w2, l)
        if not last:
            xc = _stream_back(front_c, y_c, lw, False)
        xl = _stream_back(front_l, y_l, lw, last)
    return xl.reshape(batch, seq, D_MODEL)
```

```python
import functools
import math

import numpy as np
import jax
import jax.numpy as jnp
from jax import lax
from jax.experimental import pallas as pl
from jax.experimental.pallas import tpu as pltpu

F32, BF16, I32 = jnp.float32, jnp.bfloat16, jnp.int32
HIGHEST = lax.Precision.HIGHEST

D_MODEL = 1024
GRID_W = 64
N_MOD = 6
EPS = 1e-6
NEG_INF = -1e30
NA_HEAD_DIM = 64
D_NA = 512
NA_HEADS = 8
NA_WIN_H = 8
NA_WIN_W = 16
D_HY = 256
HY_EMB = 33
HY_FAST_DECAY = 0.3
HY_SLOW_DECAY = 1.5
HY_TARGET = 1e-2
D_CF = 256
CF_KERNEL = 31
D_QKV = 3 * D_NA
V_SLOT = 128
D_QKVX = 2 * D_NA + NA_HEADS * V_SLOT
Q_SCALE = NA_HEAD_DIM ** -0.5 * math.log2(math.e)
D_HYIN = 3 * D_HY
D_CFIN = 2 * D_CF
N_EXPERTS = 16
EC_CAPACITY = 2
EXPERT_FF = 2 * D_MODEL

NA_QROWS = 4
NA_KROWS = NA_QROWS + NA_WIN_H - 1
HY_T = 256
MIB = 1024 * 1024
VMEM_LIMIT = 56 * MIB


def _params(sem, vmem=VMEM_LIMIT):
    return pltpu.CompilerParams(dimension_semantics=sem, vmem_limit_bytes=vmem)


def _nt_dot(a, b):
    return lax.dot_general(a, b, (((1,), (1,)), ((), ())), preferred_element_type=F32)


def _tn_dot(a, b):
    return lax.dot_general(a, b, (((0,), (0,)), ((), ())), preferred_element_type=F32)


def _mod_kernel(c_ref, w_ref, b_ref, o_ref):
    c = c_ref[...]
    s = c * jax.nn.sigmoid(c)
    o_ref[0] = jnp.dot(s, w_ref[0], preferred_element_type=F32, precision=HIGHEST) + b_ref[0]


def _modulation(cvec, w_mod, b_mod):
    depth, _, n = w_mod.shape
    rows = cvec.shape[0]
    tn = 1536
    return pl.pallas_call(
        _mod_kernel,
        grid=(depth, n // tn),
        in_specs=[pl.BlockSpec((rows, D_MODEL), lambda l, j: (0, 0)),
                  pl.BlockSpec((1, D_MODEL, tn), lambda l, j: (l, 0, j)),
                  pl.BlockSpec((1, 1, tn), lambda l, j: (l, 0, j))],
        out_specs=pl.BlockSpec((1, rows, tn), lambda l, j: (l, 0, j)),
        out_shape=jax.ShapeDtypeStruct((depth, rows, n), F32),
        compiler_params=_params(("arbitrary", "arbitrary")),
        name="adaln_mod",
    )(cvec, w_mod, b_mod.reshape(depth, 1, n))


def _rms_mod(x, g, sc, sh):
    y = x * lax.rsqrt(jnp.mean(x * x, axis=-1, keepdims=True) + EPS)
    return (y * g) * (1.0 + sc) + sh


def _inproj_kernel(x_ref, g_ref, sc_ref, sh_ref, w_ref, qkv_ref, hy_ref, cf_ref):
    h = _rms_mod(x_ref[...], g_ref[...], sc_ref[0], sh_ref[0]).astype(BF16)
    qkv_ref[:, :D_NA] = (jnp.dot(h, w_ref[:, :D_NA], preferred_element_type=F32) * Q_SCALE).astype(BF16)
    qkv_ref[:, D_NA:2 * D_NA] = jnp.dot(h, w_ref[:, D_NA:2 * D_NA], preferred_element_type=F32).astype(BF16)
    v = jnp.dot(h, w_ref[:, 2 * D_NA:D_QKV], preferred_element_type=F32).astype(BF16)
    lane = lax.broadcasted_iota(I32, (v.shape[0], V_SLOT - NA_HEAD_DIM), 1)
    ones_col = jnp.where(lane == 0, 1.0, 0.0).astype(BF16)
    for hd in range(NA_HEADS):
        lo = 2 * D_NA + hd * V_SLOT
        qkv_ref[:, lo:lo + NA_HEAD_DIM] = v[:, hd * NA_HEAD_DIM:(hd + 1) * NA_HEAD_DIM]
        qkv_ref[:, lo + NA_HEAD_DIM:lo + V_SLOT] = ones_col
    hy_ref[...] = jnp.dot(h, w_ref[:, D_QKV:D_QKV + D_HYIN], preferred_element_type=F32)
    cf_ref[...] = jnp.dot(h, w_ref[:, D_QKV + D_HYIN:], preferred_element_type=F32)


def _per_sample_spec(arr, tiles_per_sample):
    if arr.shape[0] == 1:
        return pl.BlockSpec((1, 1, D_MODEL), lambda i: (0, 0, 0))
    return pl.BlockSpec((1, 1, D_MODEL), lambda i: (i // tiles_per_sample, 0, 0))


def _in_projection(x2d, g, sc, sh, w_bf, seq, tm):
    n = x2d.shape[0]
    tps = seq // tm
    d_in = w_bf.shape[1]
    row = lambda w: pl.BlockSpec((tm, w), lambda i: (i, 0))
    return pl.pallas_call(
        _inproj_kernel,
        grid=(n // tm,),
        in_specs=[row(D_MODEL), pl.BlockSpec((1, D_MODEL), lambda i: (0, 0)),
                  _per_sample_spec(sc, tps), _per_sample_spec(sh, tps),
                  pl.BlockSpec((D_MODEL, d_in), lambda i: (0, 0))],
        out_specs=[row(D_QKVX), row(D_HYIN), row(D_CFIN)],
        out_shape=[jax.ShapeDtypeStruct((n, D_QKVX), BF16), jax.ShapeDtypeStruct((n, D_HYIN), F32),
                   jax.ShapeDtypeStruct((n, D_CFIN), F32)],
        compiler_params=_params(("parallel",)),
        name="in_projection",
    )(x2d, g, sc, sh, w_bf)


def _na_bias_tiles(rpb, rows):
    col = np.arange(GRID_W)
    c0 = np.clip(col - NA_WIN_W // 2, 0, GRID_W - NA_WIN_W)
    col_ok = (col[None, :] >= c0[:, None]) & (col[None, :] < c0[:, None] + NA_WIN_W)
    dc = np.clip(col[None, :] - col[:, None], 1 - NA_WIN_W, NA_WIN_W - 1) + NA_WIN_W - 1
    n_groups = rows // NA_QROWS
    n_dr, n_dc = 2 * NA_WIN_H - 1, 2 * NA_WIN_W - 1
    dr_hot = np.zeros((3, NA_QROWS, NA_KROWS, n_dr), np.float32)
    ok = np.zeros((3, NA_QROWS, GRID_W, NA_KROWS, GRID_W), bool)
    for v, grp in enumerate((0, 1, n_groups - 1)):
        r = grp * NA_QROWS + np.arange(NA_QROWS)
        k0 = np.clip(grp * NA_QROWS - NA_WIN_H // 2, 0, rows - NA_KROWS)
        kr = k0 + np.arange(NA_KROWS)
        r0 = np.clip(r - NA_WIN_H // 2, 0, rows - NA_WIN_H)
        row_ok = (kr[None, :] >= r0[:, None]) & (kr[None, :] < r0[:, None] + NA_WIN_H)
        dr = kr[None, :] - r[:, None] + NA_WIN_H - 1
        for i, j in zip(*np.nonzero(row_ok)):
            dr_hot[v, i, j, dr[i, j]] = 1.0
        ok[v] = row_ok[:, None, :, None] & col_ok[None, :, None, :]
    dc_hot = (dc[None, :, :] == np.arange(n_dc)[:, None, None]).astype(np.float32)
    t1 = jnp.einsum('hrd,dqk->hrqk', rpb.astype(F32), jnp.asarray(dc_hot), precision=HIGHEST)
    bias = jnp.einsum('vijr,hrqk->vhiqjk', jnp.asarray(dr_hot), t1, precision=HIGHEST)
    bias = jnp.where(jnp.asarray(ok)[:, None], bias * math.log2(math.e), NEG_INF)
    return bias.reshape(3, NA_HEADS, NA_QROWS * GRID_W, NA_KROWS * GRID_W)


def _softmax_pv(s_list, v_list):
    m = functools.reduce(jnp.maximum, [jnp.max(s, axis=-1, keepdims=True) for s in s_list])
    o = functools.reduce(jnp.add, [jnp.dot(jnp.exp2(s - m).astype(BF16), v, preferred_element_type=F32)
                                   for s, v in zip(s_list, v_list)])
    return o[:, :NA_HEAD_DIM] / o[:, NA_HEAD_DIM:NA_HEAD_DIM + 1]


def _na_kernel(q_ref, k_ref, v_ref, kc_ref, vc_ref, bias_ref, o_ref, *, rows):
    g = pl.program_id(1)
    k0 = jnp.clip(g * NA_QROWS - NA_WIN_H // 2, 0, rows - NA_KROWS) * GRID_W
    k0 = pl.multiple_of(k0, GRID_W)
    nk = NA_KROWS * GRID_W
    def scores(h):
        sl = slice(h * NA_HEAD_DIM, (h + 1) * NA_HEAD_DIM)
        qh = q_ref[:, sl]
        return [_nt_dot(qh, k_ref[pl.ds(k0, nk), sl]) + bias_ref[0, h], _nt_dot(qh, kc_ref[:, sl])]

    s = scores(0)
    for h in range(NA_HEADS):
        s_next = scores(h + 1) if h + 1 < NA_HEADS else None
        vs = slice(h * V_SLOT, (h + 1) * V_SLOT)
        o = _softmax_pv(s, [v_ref[pl.ds(k0, nk), vs], vc_ref[:, vs]])
        o_ref[:, h * NA_HEAD_DIM:(h + 1) * NA_HEAD_DIM] = o.astype(o_ref.dtype)
        s = s_next


def _neighbourhood_attention(qkv, qkv_c, bias, batch, seq, ctx_len):
    rows = seq // GRID_W
    n_groups = rows // NA_QROWS
    nq, nk = NA_QROWS * GRID_W, NA_KROWS * GRID_W
    sel = lambda g: jnp.where(g == 0, 0, jnp.where(g == n_groups - 1, 2, 1))
    return pl.pallas_call(
        functools.partial(_na_kernel, rows=rows),
        grid=(batch, n_groups),
        in_specs=[pl.BlockSpec((nq, D_NA), lambda b, g: (b * n_groups + g, 0)),
                  pl.BlockSpec((seq, D_NA), lambda b, g: (b, 1)),
                  pl.BlockSpec((seq, NA_HEADS * V_SLOT), lambda b, g: (b, 1)),
                  pl.BlockSpec((ctx_len, D_NA), lambda b, g: (b, 1)),
                  pl.BlockSpec((ctx_len, NA_HEADS * V_SLOT), lambda b, g: (b, 1)),
                  pl.BlockSpec((1, NA_HEADS, nq, nk), lambda b, g: (sel(g), 0, 0, 0))],
        out_specs=pl.BlockSpec((nq, D_NA), lambda b, g: (b * n_groups + g, 0)),
        out_shape=jax.ShapeDtypeStruct((batch * seq, D_NA), BF16),
        compiler_params=_params(("parallel", "arbitrary")),
        name="neighbourhood_attention",
    )(qkv, qkv, qkv, qkv_c, qkv_c, bias)


def _ctx_attn_kernel(q_ref, k_ref, v_ref, o_ref):
    for h in range(NA_HEADS):
        sl = slice(h * NA_HEAD_DIM, (h + 1) * NA_HEAD_DIM)
        o = _softmax_pv([_nt_dot(q_ref[:, sl], k_ref[:, sl])], [v_ref[:, h * V_SLOT:(h + 1) * V_SLOT]])
        o_ref[:, sl] = o.astype(o_ref.dtype)


def _context_attention(qkv_c, batch, ctx_len):
    spec = lambda j: pl.BlockSpec((ctx_len, D_NA), lambda b: (b, j))
    return pl.pallas_call(
        _ctx_attn_kernel,
        grid=(batch,),
        in_specs=[spec(0), spec(1), pl.BlockSpec((ctx_len, NA_HEADS * V_SLOT), lambda b: (b, 1))],
        out_specs=spec(0),
        out_shape=jax.ShapeDtypeStruct((batch * ctx_len, D_NA), BF16),
        compiler_params=_params(("parallel",)),
        name="context_attention",
    )(qkv_c, qkv_c, qkv_c)


SUBLANES = 8
LANES = 128
SHORT_CHUNK = 128
DW_CHUNK = 128
DW_STAGES = 2
HY_HALO = 8
CF_HALO = 16


def _conv_kernel(hy_ref, cf_ref, sw_ref, sb_ref, dw_ref, db_ref, lg_ref, lb_ref,
                 x1_ref, x2_ref, v_ref, cfo_ref, hpad, gpad, ph_scr, *, seq):
    hpad[0:HY_HALO, :] = jnp.zeros((HY_HALO, D_HYIN), F32)
    hpad[seq + HY_HALO:seq + 2 * HY_HALO, :] = jnp.zeros((HY_HALO, D_HYIN), F32)
    hpad[HY_HALO:seq + HY_HALO, :] = hy_ref[...]
    gpad[0:CF_HALO, :] = jnp.zeros((CF_HALO, D_CF), F32)
    gpad[seq + CF_HALO:seq + 2 * CF_HALO, :] = jnp.zeros((CF_HALO, D_CF), F32)
    gpad[CF_HALO:seq + CF_HALO, :] = cf_ref[:, :D_CF] * jax.nn.sigmoid(cf_ref[:, D_CF:])
    pad_s = 1
    pad_c = (CF_KERNEL - 1) // 2

    def short_body(i, carry):
        ch = SHORT_CHUNK
        base = pl.multiple_of(i * ch, ch)
        win = hpad[pl.ds(base, ch + 2 * HY_HALO), :]
        u = sb_ref[...]
        for t in range(3):
            o = HY_HALO - pad_s + t
            u = u + win[o:o + ch, :] * sw_ref[t:t + 1, :]
        x1_ref[i] = u[:, :D_HY].T
        x2_ref[i] = u[:, D_HY:2 * D_HY].T
        v_ref[i] = u[:, 2 * D_HY:].T
        return carry

    lax.fori_loop(0, seq // SHORT_CHUNK, short_body, 0)

    def dw_chunk(base, stage):
        ch = DW_CHUNK
        gw = gpad[pl.ds(base, ch + 2 * CF_HALO), :]
        span = ch + 2 * CF_HALO - SUBLANES
        for phase in range(1, SUBLANES):
            ph_scr[stage, phase - 1] = gw[phase:phase + span, :]
        y = db_ref[...]
        for t in range(CF_KERNEL):
            o = CF_HALO - pad_c + t
            phase = o % SUBLANES
            if phase == 0:
                src = gpad[pl.ds(base + o, ch), :]
            else:
                src = ph_scr[stage, phase - 1, o - phase:o - phase + ch, :]
            y = y + src * dw_ref[t:t + 1, :]
        mu = jnp.mean(y, axis=-1, keepdims=True)
        yc = y - mu
        var = jnp.mean(yc * yc, axis=-1, keepdims=True)
        yn = yc * lax.rsqrt(var + EPS) * lg_ref[...] + lb_ref[...]
        cfo_ref[pl.ds(base, ch), :] = (yn * jax.nn.sigmoid(yn)).astype(cfo_ref.dtype)

    def dw_body(i, carry):
        for stage in range(DW_STAGES):
            dw_chunk(pl.multiple_of((i * DW_STAGES + stage) * DW_CHUNK, DW_CHUNK), stage)
        return carry

    lax.fori_loop(0, seq // (DW_CHUNK * DW_STAGES), dw_body, 0)


def _conv_mixers(hy_in, cf_in, sw, sb, dw, db, lg, lb, batch, seq):
    full = lambda a: pl.BlockSpec(a.shape, lambda b: (0, 0))
    cm = pl.BlockSpec((seq // LANES, None, D_HY, LANES), lambda b: (0, b, 0, 0))
    cm_shape = jax.ShapeDtypeStruct((seq // LANES, batch, D_HY, LANES), F32)
    return pl.pallas_call(
        functools.partial(_conv_kernel, seq=seq),
        grid=(batch,),
        in_specs=[pl.BlockSpec((seq, D_HYIN), lambda b: (b, 0)), pl.BlockSpec((seq, D_CFIN), lambda b: (b, 0)),
                  full(sw), full(sb), full(dw), full(db), full(lg), full(lb)],
        out_specs=[cm, cm, cm, pl.BlockSpec((seq, D_CF), lambda b: (b, 0))],
        out_shape=[cm_shape, cm_shape, cm_shape, jax.ShapeDtypeStruct((batch * seq, D_CF), BF16)],
        scratch_shapes=[pltpu.VMEM((seq + 2 * HY_HALO, D_HYIN), F32), pltpu.VMEM((seq + 2 * CF_HALO, D_CF), F32),
                        pltpu.VMEM((DW_STAGES, SUBLANES - 1, DW_CHUNK + 2 * CF_HALO - SUBLANES, D_CF), F32)],
        compiler_params=_params(("parallel",)),
        name="conv_mixers",
    )(hy_in, cf_in, sw, sb, dw, db, lg, lb)


def _filter_kernel(fb_ref, w1t_ref, w1c_ref, w1s_ref, b1_ref, fr_ref, w2_ref, b2_ref, w3_ref, rate_ref, o_ref, *, seq):
    pos = lax.broadcasted_iota(I32, (seq, 1), 0).astype(F32)
    t = pos / max(seq - 1, 1)
    ang = fb_ref[...] * (2.0 * math.pi) * pos / seq
    dot = functools.partial(jnp.dot, preferred_element_type=F32, precision=HIGHEST)
    fr = fr_ref[...]
    z1 = t * w1t_ref[...] + dot(jnp.cos(ang), w1c_ref[...]) + dot(-jnp.sin(ang), w1s_ref[...]) + b1_ref[...]
    h = jnp.sin(fr * z1)
    h = jnp.sin(fr * (dot(h, w2_ref[...]) + b2_ref[...]))
    o_ref[...] = dot(h, w3_ref[...]) * jnp.exp(-t * rate_ref[...])


def _hyena_filters(seq, w1, b1, freq, w2, b2, w3):
    bands = (HY_EMB - 1) // 2
    fb = jnp.linspace(1e-4, bands - 1, bands, dtype=F32)[None, :]
    min_decay = math.log(HY_TARGET) / HY_SLOW_DECAY
    max_decay = math.log(HY_TARGET) / HY_FAST_DECAY
    rate = jnp.abs(jnp.linspace(min_decay, max_decay, D_HY, dtype=F32))
    rate4 = jnp.tile(rate, 4)[None, :]
    args = (fb, w1[0:1], w1[1:1 + bands], w1[1 + bands:], b1[None, :], freq[None, :], w2, b2[None, :], w3, rate4)
    h = pl.pallas_call(
        functools.partial(_filter_kernel, seq=seq),
        out_shape=jax.ShapeDtypeStruct((seq, 4 * D_HY), F32),
        compiler_params=_params(None),
        name="hyena_filter",
    )(*args)
    h = h.reshape(seq, 2, 2, D_HY)
    fwd, bwd = h[:, 0], h[:, 1]
    line = jnp.concatenate([jnp.zeros_like(fwd[:1]), bwd[1:][::-1], fwd], axis=0)
    return jnp.transpose(line, (1, 2, 0))


HY_CG = 8


def _hyena_kernel(d_ref, x1_ref, x2_ref, v_ref, kf_ref, o_ref, w_scr, acc_scr, *, batch, seq):
    nb = seq // HY_T
    tiles = HY_T // LANES
    cg = pl.program_id(0)

    def rows_of(ref):
        return ref.reshape(seq // LANES * batch * HY_CG, LANES)

    def tile_rows(k, c):
        return pl.ds((k * batch) * HY_CG + c, batch, stride=HY_CG)

    def blocks(ref, c):
        rows = rows_of(ref)
        return jnp.concatenate(
            [jnp.concatenate([rows[tile_rows(j * tiles + h, c), :] for h in range(tiles)], axis=1)
             for j in range(nb)], axis=0)

    out_rows = rows_of(o_ref)
    for c in range(HY_CG):
        z = blocks(v_ref, c)
        for n, gate_ref in enumerate((x1_ref, x2_ref)):
            line = jnp.broadcast_to(kf_ref[n, c:c + 1, :], (HY_T, 2 * seq))
            w_scr[...] = pltpu.roll(line, 0, 1, stride=1, stride_axis=0).astype(BF16)
            zb = z.astype(BF16)
            acc_scr[...] = jnp.dot(zb, w_scr[:, seq:seq + HY_T], preferred_element_type=F32)
            for d in range(1, nb):
                m = batch * (nb - d)
                acc_scr[batch * d:, :] += jnp.dot(zb[:m], w_scr[:, seq + d * HY_T:seq + (d + 1) * HY_T],
                                                  preferred_element_type=F32)
                acc_scr[:m, :] += jnp.dot(zb[batch * d:], w_scr[:, seq - d * HY_T:seq - (d - 1) * HY_T],
                                          preferred_element_type=F32)
            z = blocks(gate_ref, c) * (acc_scr[...] + d_ref[n, cg * HY_CG + c] * z)
        for j in range(nb):
            for h in range(tiles):
                out_rows[tile_rows(j * tiles + h, c), :] = z[j * batch:(j + 1) * batch, h * LANES:(h + 1) * LANES]


def _hyena_long_conv(x1t, x2t, vt, kf, bias_d, batch, seq):
    cm = pl.BlockSpec((seq // LANES, batch, HY_CG, LANES), lambda g: (0, 0, g, 0))
    return pl.pallas_call(
        functools.partial(_hyena_kernel, batch=batch, seq=seq),
        grid=(D_HY // HY_CG,),
        in_specs=[pl.BlockSpec(memory_space=pltpu.SMEM), cm, cm, cm,
                  pl.BlockSpec((2, HY_CG, 2 * seq), lambda g: (0, g, 0))],
        out_specs=cm,
        out_shape=jax.ShapeDtypeStruct((seq // LANES, batch, D_HY, LANES), F32),
        scratch_shapes=[pltpu.VMEM((HY_T, 2 * seq), BF16), pltpu.VMEM((seq // HY_T * batch, HY_T), F32)],
        compiler_params=_params(("parallel",)),
        name="hyena_long_conv",
    )(bias_d, x1t, x2t, vt, kf)


def _outproj_kernel(attn_ref, hy_ref, cf_ref, x_ref, g1_ref, w_ref, ng_ref, sc_ref, sh_ref, rw_ref,
                    xo_ref, h2_ref, lt_ref):
    y = jnp.dot(attn_ref[...], w_ref[:D_NA], preferred_element_type=F32)
    hy = jnp.concatenate([hy_ref[k].T for k in range(hy_ref.shape[0])], axis=0).astype(BF16)
    y = y + jnp.dot(hy, w_ref[D_NA:D_NA + D_HY], preferred_element_type=F32)
    y = y + jnp.dot(cf_ref[...], w_ref[D_NA + D_HY:], preferred_element_type=F32)
    xn = x_ref[...] + g1_ref[0] * y
    xo_ref[...] = xn
    h2 = _rms_mod(xn, ng_ref[...], sc_ref[0], sh_ref[0]).astype(BF16)
    h2_ref[...] = h2
    lt_ref[0] = _nt_dot(rw_ref[...], h2)


def _out_projection(attn, hyt, cf, x2d, g1, w_bf, ng, sc, sh, rw_t, batch, seq, tm):
    n = x2d.shape[0]
    tps = seq // tm
    row = lambda w: pl.BlockSpec((tm, w), lambda i: (i, 0))
    full = lambda a: pl.BlockSpec(a.shape, lambda i: (0, 0))
    return pl.pallas_call(
        _outproj_kernel,
        grid=(n // tm,),
        in_specs=[row(D_NA), pl.BlockSpec((tm // LANES, None, D_HY, LANES), lambda i: (i % tps, i // tps, 0, 0)),
                  row(D_CF), row(D_MODEL),
                  _per_sample_spec(g1, tps), full(w_bf), full(ng), _per_sample_spec(sc, tps),
                  _per_sample_spec(sh, tps), full(rw_t)],
        out_specs=[row(D_MODEL), row(D_MODEL), pl.BlockSpec((1, N_EXPERTS, tm), lambda i: (i // tps, 0, i % tps))],
        out_shape=[jax.ShapeDtypeStruct((n, D_MODEL), F32), jax.ShapeDtypeStruct((n, D_MODEL), BF16),
                   jax.ShapeDtypeStruct((batch, N_EXPERTS, seq), F32)],
        compiler_params=_params(("parallel",)),
        name="out_projection",
    )(attn, hyt, cf, x2d, g1, w_bf, ng, sc, sh, rw_t)


def _route_kernel(lt_ref, tri_ref, pos_ref, aff_ref, *, cap):
    lg = lt_ref[...]
    bs, _, seq = lg.shape
    e = jnp.exp(lg - jnp.max(lg, axis=1, keepdims=True))
    aff = e / jnp.sum(e, axis=1, keepdims=True)
    aff_ref[...] = aff
    keys = pltpu.bitcast(aff.reshape(bs * N_EXPERTS, seq), I32)

    def body(i, thr):
        cand = thr | (jnp.int32(1) << (30 - i))
        cnt = jnp.sum(jnp.where(keys >= cand, 1.0, 0.0), axis=1, keepdims=True)
        return jnp.where(cnt >= cap, cand, thr)

    thr = lax.fori_loop(0, 31, body, jnp.zeros((bs * N_EXPERTS, 1), I32))
    gt = keys > thr
    eq = keys == thr
    need = cap - jnp.sum(jnp.where(gt, 1.0, 0.0), axis=1, keepdims=True)
    rank = jnp.dot(jnp.where(eq, 1.0, 0.0).astype(BF16), tri_ref[...], preferred_element_type=F32)
    sel = gt | (eq & (rank < need))
    slot = jnp.dot(jnp.where(sel, 1.0, 0.0).astype(BF16), tri_ref[...], preferred_element_type=F32)
    pos_ref[...] = jnp.where(sel, slot, -1.0).astype(I32).reshape(bs, N_EXPERTS, seq)


def _routing(logits_t, tri, cap):
    batch, _, seq = logits_t.shape
    bs = 8 if batch % 8 == 0 else batch
    blk = pl.BlockSpec((bs, N_EXPERTS, seq), lambda i: (i, 0, 0))
    return pl.pallas_call(
        functools.partial(_route_kernel, cap=cap),
        grid=(batch // bs,),
        in_specs=[blk, pl.BlockSpec((seq, seq), lambda i: (0, 0))],
        out_specs=[blk, blk],
        out_shape=[jax.ShapeDtypeStruct((batch, N_EXPERTS, seq), I32),
                   jax.ShapeDtypeStruct((batch, N_EXPERTS, seq), F32)],
        compiler_params=_params(("parallel",)),
        name="ec_routing",
    )(logits_t, tri)


def _gather_kernel(pos_ref, h_ref, xe_ref, *, cap):
    seq = h_ref.shape[0]
    slot = lax.broadcasted_iota(I32, (cap, seq), 0)
    for e in range(N_EXPERTS):
        onehot = jnp.where(pos_ref[0, e:e + 1, :] == slot, 1.0, 0.0).astype(BF16)
        xe_ref[0, e * cap:(e + 1) * cap, :] = jnp.dot(onehot, h_ref[...], preferred_element_type=F32).astype(BF16)


def _gather_tokens(pos, h2, cap):
    batch, _, seq = pos.shape
    return pl.pallas_call(
        functools.partial(_gather_kernel, cap=cap),
        grid=(batch,),
        in_specs=[pl.BlockSpec((1, N_EXPERTS, seq), lambda b: (b, 0, 0)),
                  pl.BlockSpec((seq, D_MODEL), lambda b: (b, 0))],
        out_specs=pl.BlockSpec((1, N_EXPERTS * cap, D_MODEL), lambda b: (b, 0, 0)),
        out_shape=jax.ShapeDtypeStruct((batch, N_EXPERTS * cap, D_MODEL), BF16),
        compiler_params=_params(("parallel",)),
        name="ec_gather",
    )(pos, h2)


FFN_ROWS = 512


def _ffn_kernel(*refs, layer, n_ctx_tiles):
    if n_ctx_tiles:
        xe_ref, xc_ref, w1_hbm, w3_hbm, w2_hbm, y_ref, yc_ref, w1b, w3b, w2b, st1, st3, st2, sem = refs
    else:
        xe_ref, w1_hbm, w3_hbm, w2_hbm, y_ref, w1b, w3b, w2b, st1, st3, st2, sem = refs
    e, rt = pl.program_id(0), pl.program_id(1)
    n_e, n_rt = N_EXPERTS, pl.num_programs(1)
    r1, r2 = st1.shape[1], st2.shape[1]

    def slab_copies(expert, k, slot):
        o1, o2 = pl.multiple_of(k * r1, r1), pl.multiple_of(k * r2, r2)
        return (pltpu.make_async_copy(w1_hbm.at[layer, expert, pl.ds(o1, r1), :], st1.at[slot], sem.at[slot, 0]),
                pltpu.make_async_copy(w3_hbm.at[layer, expert, pl.ds(o1, r1), :], st3.at[slot], sem.at[slot, 1]),
                pltpu.make_async_copy(w2_hbm.at[layer, expert, pl.ds(o2, r2), :], st2.at[slot], sem.at[slot, 2]))

    def start_slab(expert, k, slot):
        for cp in slab_copies(expert, k, slot):
            cp.start()

    def land_slab(expert, k, slot, wslot):
        for cp in slab_copies(expert, k, slot):
            cp.wait()
        o1, o2 = pl.multiple_of(k * r1, r1), pl.multiple_of(k * r2, r2)
        w1b[wslot, pl.ds(o1, r1), :] = st1[slot].astype(BF16)
        w3b[wslot, pl.ds(o1, r1), :] = st3[slot].astype(BF16)
        w2b[wslot, pl.ds(o2, r2), :] = st2[slot].astype(BF16)

    def prefetched_expert(expert):
        return jnp.minimum(expert + 1, n_e - 1)

    n_slabs = D_MODEL // r1

    @pl.when((e == 0) & (rt == 0))
    def _():
        start_slab(0, 0, 0)
        for k in range(n_slabs):
            if k + 1 < n_slabs:
                start_slab(0, k + 1, (k + 1) % 2)
            land_slab(0, k, k % 2, 0)
        start_slab(prefetched_expert(0), 0, 0)

    land_slab(prefetched_expert(e), rt, rt % 2, (e + 1) % 2)
    ws = e % 2

    def swiglu(x_ref, o_ref):
        bt, cap, _ = x_ref.shape
        x = x_ref[...].reshape(bt * cap, D_MODEL)
        a = jnp.dot(x, w1b[ws], preferred_element_type=F32)
        u = jnp.dot(x, w3b[ws], preferred_element_type=F32)
        h = (a * jax.nn.sigmoid(a) * u).astype(BF16)
        o_ref[...] = jnp.dot(h, w2b[ws], preferred_element_type=F32).astype(BF16).reshape(bt, cap, D_MODEL)

    swiglu(xe_ref, y_ref)
    if n_ctx_tiles:
        @pl.when(rt < n_ctx_tiles)
        def _():
            swiglu(xc_ref, yc_ref)

    last_tile = rt + 1 == n_rt
    nxt_e = jnp.where(last_tile, e + 1, e)
    nxt_k = jnp.where(last_tile, 0, rt + 1)

    @pl.when(nxt_e < n_e)
    def _():
        start_slab(prefetched_expert(nxt_e), nxt_k, nxt_k % 2)


def _tile_samples(batch, cap):
    bt = max(1, min(batch, FFN_ROWS // cap))
    while batch % bt:
        bt -= 1
    return bt


def _expert_ffn(xe, xe_ctx, w1, w3, w2, layer):
    batch = xe.shape[0]
    cap = xe.shape[1] // N_EXPERTS
    bt = _tile_samples(batch, cap)
    n_rt = batch // bt
    xspec = pl.BlockSpec((bt, cap, D_MODEL), lambda e, r: (r, e, 0))
    hbm = pl.BlockSpec(memory_space=pl.ANY)
    in_specs, out_specs, out_shape, args, n_ctx_tiles = [xspec], [xspec], [jax.ShapeDtypeStruct(xe.shape, BF16)], [xe], 0
    if xe_ctx is not None:
        cap_c = xe_ctx.shape[1] // N_EXPERTS
        bt_c = _tile_samples(batch, cap_c)
        n_ctx_tiles = batch // bt_c
        assert n_ctx_tiles <= n_rt
        cspec = pl.BlockSpec((bt_c, cap_c, D_MODEL), lambda e, r: (jnp.minimum(r, n_ctx_tiles - 1), e, 0))
        in_specs.append(cspec)
        out_specs.append(cspec)
        out_shape.append(jax.ShapeDtypeStruct(xe_ctx.shape, BF16))
        args.append(xe_ctx)
    assert D_MODEL % n_rt == 0 and EXPERT_FF % n_rt == 0
    r1, r2 = D_MODEL // n_rt, EXPERT_FF // n_rt
    out = pl.pallas_call(
        functools.partial(_ffn_kernel, layer=layer, n_ctx_tiles=n_ctx_tiles),
        grid=(N_EXPERTS, n_rt),
        in_specs=in_specs + [hbm, hbm, hbm],
        out_specs=out_specs,
        out_shape=out_shape,
        scratch_shapes=[pltpu.VMEM((2, D_MODEL, EXPERT_FF), BF16), pltpu.VMEM((2, D_MODEL, EXPERT_FF), BF16),
                        pltpu.VMEM((2, EXPERT_FF, D_MODEL), BF16), pltpu.VMEM((2, r1, EXPERT_FF), F32),
                        pltpu.VMEM((2, r1, EXPERT_FF), F32), pltpu.VMEM((2, r2, D_MODEL), F32),
                        pltpu.SemaphoreType.DMA((2, 3))],
        compiler_params=_params(("arbitrary", "arbitrary")),
        name="ec_expert_ffn",
    )(*args, w1, w3, w2)
    return (out[0], out[1]) if xe_ctx is not None else (out[0], None)


def _scatter_kernel(pos_ref, aff_ref, y_ref, x_ref, g2_ref, fg_ref, o_ref, pg_scr, *, cap, final_norm):
    tl = x_ref.shape[0]
    slot = lax.broadcasted_iota(I32, (cap, tl), 0)
    for e in range(N_EXPERTS):
        hit = pos_ref[0, e:e + 1, :] == slot
        pg_scr[e * cap:(e + 1) * cap, :] = jnp.where(hit, aff_ref[0, e:e + 1, :], 0.0).astype(BF16)
    moe = _tn_dot(pg_scr[...], y_ref[0])
    xn = x_ref[...] + g2_ref[0] * moe
    if final_norm:
        xn = xn * lax.rsqrt(jnp.mean(xn * xn, axis=-1, keepdims=True) + EPS) * fg_ref[...]
    o_ref[...] = xn


def _scatter_residual(pos, aff, y, x2d, g2, fg, cap, tl, final_norm):
    batch, _, seq = pos.shape
    tps = seq // tl
    rspec = pl.BlockSpec((1, N_EXPERTS, tl), lambda b, t: (b, 0, t))
    xspec = pl.BlockSpec((tl, D_MODEL), lambda b, t: (b * tps + t, 0))
    g2spec = (pl.BlockSpec((1, 1, D_MODEL), lambda b, t: (0, 0, 0)) if g2.shape[0] == 1
              else pl.BlockSpec((1, 1, D_MODEL), lambda b, t: (b, 0, 0)))
    return pl.pallas_call(
        functools.partial(_scatter_kernel, cap=cap, final_norm=final_norm),
        grid=(batch, tps),
        in_specs=[rspec, rspec, pl.BlockSpec((1, N_EXPERTS * cap, D_MODEL), lambda b, t: (b, 0, 0)), xspec,
                  g2spec, pl.BlockSpec((1, D_MODEL), lambda b, t: (0, 0))],
        out_specs=xspec,
        out_shape=jax.ShapeDtypeStruct(x2d.shape, F32),
        scratch_shapes=[pltpu.VMEM((N_EXPERTS * cap, tl), BF16)],
        compiler_params=_params(("parallel", "arbitrary")),
        name="ec_scatter_residual",
    )(pos, aff, y, x2d, g2, fg)


def _strict_lower_ones(n):
    i = np.arange(n)
    return jnp.asarray(i[:, None] < i[None, :], dtype=BF16)


def _stream_front(x2d, mods, lw, batch, seq, tm, attn_fn):
    sh1, sc1, g1, sh2, sc2, g2 = mods
    qkv, hy_in, cf_in = _in_projection(x2d, lw["norm1_g"], sc1, sh1, lw["w_in"], seq, tm)
    attn = attn_fn(qkv)
    x1t, x2t, vt, cf = _conv_mixers(hy_in, cf_in, lw["hy_short_w"], lw["hy_short_b"], lw["cf_dw_w"], lw["cf_dw_b"],
                                    lw["cf_ln_g"], lw["cf_ln_b"], batch, seq)
    kf = _hyena_filters(seq, lw["hy_filt_w1"], lw["hy_filt_b1"], lw["hy_filt_freq"], lw["hy_filt_w2"],
                        lw["hy_filt_b2"], lw["hy_filt_w3"])
    hyt = _hyena_long_conv(x1t, x2t, vt, kf, lw["hy_bias_d"], batch, seq)
    x_mid, h2, logits_t = _out_projection(attn, hyt, cf, x2d, g1, lw["w_out"], lw["norm2_g"], sc2, sh2,
                                          lw["router_wt"], batch, seq, tm)
    cap = EC_CAPACITY * seq // N_EXPERTS
    pos, aff = _routing(logits_t, _strict_lower_ones(seq), cap)
    xe = _gather_tokens(pos, h2, cap)
    return dict(pos=pos, aff=aff, xe=xe, x_mid=x_mid, g2=g2, cap=cap, seq=seq, qkv=qkv)


def _stream_back(front, y, lw, final_norm):
    return _scatter_residual(front["pos"], front["aff"], y, front["x_mid"], front["g2"], lw["final_norm_g"],
                             front["cap"], min(front["seq"], 512), final_norm)


def kernel(x, c, ctx, c_ctx, w_mod, b_mod, norm1_g, norm2_g, w_in, na_rpb, hy_short_w, hy_short_b, hy_filt_w1,
           hy_filt_b1, hy_filt_freq, hy_filt_w2, hy_filt_b2, hy_filt_w3, hy_bias_d, cf_dw_w, cf_dw_b, cf_ln_g,
           cf_ln_b, w_out, router_w, expert_w1, expert_w3, expert_w2, final_norm_g):
    batch, seq, _ = x.shape
    ctx_len = ctx.shape[1]
    depth = w_mod.shape[0]
    rows = seq // GRID_W

    n_c = batch + 1
    n_c_pad = -(-n_c // 8) * 8
    cvec = jnp.concatenate([c, c_ctx[None, :], jnp.zeros((n_c_pad - n_c, D_MODEL), F32)], axis=0)
    mod_all = _modulation(cvec, w_mod, b_mod)

    xl = x.reshape(batch * seq, D_MODEL)
    xc = ctx.reshape(batch * ctx_len, D_MODEL)
    for l in range(depth):
        last = l == depth - 1
        lw = dict(
            norm1_g=norm1_g[l][None, :], norm2_g=norm2_g[l][None, :], w_in=w_in[l].astype(BF16),
            hy_short_w=hy_short_w[l], hy_short_b=hy_short_b[l][None, :], hy_filt_w1=hy_filt_w1[l],
            hy_filt_b1=hy_filt_b1[l], hy_filt_freq=hy_filt_freq[l], hy_filt_w2=hy_filt_w2[l],
            hy_filt_b2=hy_filt_b2[l], hy_filt_w3=hy_filt_w3[l], hy_bias_d=hy_bias_d[l], cf_dw_w=cf_dw_w[l],
            cf_dw_b=cf_dw_b[l][None, :], cf_ln_g=cf_ln_g[l][None, :], cf_ln_b=cf_ln_b[l][None, :],
            w_out=w_out[l].astype(BF16), router_wt=router_w[l].T.astype(BF16),
            expert_w1=expert_w1, expert_w3=expert_w3, expert_w2=expert_w2, layer=l,
            final_norm_g=final_norm_g[None, :])
        chunks = [mod_all[l, :, j * D_MODEL:(j + 1) * D_MODEL] for j in range(N_MOD)]
        mods_l = [m[:batch, None, :] for m in chunks]
        mods_c = [m[batch:batch + 1, None, :] for m in chunks]
        bias = _na_bias_tiles(na_rpb[l], rows)

        if last:
            front_c = None
            qkv_c, _, _ = _in_projection(xc, lw["norm1_g"], mods_c[1], mods_c[0], lw["w_in"], ctx_len, ctx_len)
        else:
            front_c = _stream_front(xc, mods_c, lw, batch, ctx_len, ctx_len,
                                    lambda qkv: _context_attention(qkv, batch, ctx_len))
            qkv_c = front_c["qkv"]
        front_l = _stream_front(xl, mods_l, lw, batch, seq, 512,
                                lambda qkv: _neighbourhood_attention(qkv, qkv_c, bias, batch, seq, ctx_len))
        y_l, y_c = _expert_ffn(front_l["xe"], None if last else front_c["xe"], expert_w1, expert_w3, expert_w2, l)
        if not last:
            xc = _stream_back(front_c, y_c, lw, False)
        xl = _stream_back(front_l, y_l, lw, last)
    return xl.reshape(batch, seq, D_MODEL)
```

```python
import functools
import math

import numpy as np
import jax
import jax.numpy as jnp
from jax import lax
from jax.experimental import pallas as pl
from jax.experimental.pallas import tpu as pltpu

F32, BF16, I32 = jnp.float32, jnp.bfloat16, jnp.int32
HIGHEST = lax.Precision.HIGHEST

D_MODEL = 1024
GRID_W = 64
N_MOD = 6
EPS = 1e-6
NEG_INF = -1e30
NA_HEAD_DIM = 64
D_NA = 512
NA_HEADS = 8
NA_WIN_H = 8
NA_WIN_W = 16
D_HY = 256
HY_EMB = 33
HY_FAST_DECAY = 0.3
HY_SLOW_DECAY = 1.5
HY_TARGET = 1e-2
D_CF = 256
CF_KERNEL = 31
D_QKV = 3 * D_NA
V_SLOT = 128
D_QKVX = 2 * D_NA + NA_HEADS * V_SLOT
Q_SCALE = NA_HEAD_DIM ** -0.5 * math.log2(math.e)
D_HYIN = 3 * D_HY
D_CFIN = 2 * D_CF
N_EXPERTS = 16
EC_CAPACITY = 2
EXPERT_FF = 2 * D_MODEL

NA_QROWS = 4
NA_KROWS = NA_QROWS + NA_WIN_H - 1
HY_T = 256
MIB = 1024 * 1024
VMEM_LIMIT = 56 * MIB


def _params(sem, vmem=VMEM_LIMIT):
    return pltpu.CompilerParams(dimension_semantics=sem, vmem_limit_bytes=vmem)


def _nt_dot(a, b):
    return lax.dot_general(a, b, (((1,), (1,)), ((), ())), preferred_element_type=F32)


def _tn_dot(a, b):
    return lax.dot_general(a, b, (((0,), (0,)), ((), ())), preferred_element_type=F32)


def _mod_kernel(c_ref, w_ref, b_ref, o_ref):
    c = c_ref[...]
    s = c * jax.nn.sigmoid(c)
    o_ref[0] = jnp.dot(s, w_ref[0], preferred_element_type=F32, precision=HIGHEST) + b_ref[0]


def _modulation(cvec, w_mod, b_mod):
    depth, _, n = w_mod.shape
    rows = cvec.shape[0]
    tn = 1536
    return pl.pallas_call(
        _mod_kernel,
        grid=(depth, n // tn),
        in_specs=[pl.BlockSpec((rows, D_MODEL), lambda l, j: (0, 0)),
                  pl.BlockSpec((1, D_MODEL, tn), lambda l, j: (l, 0, j)),
                  pl.BlockSpec((1, 1, tn), lambda l, j: (l, 0, j))],
        out_specs=pl.BlockSpec((1, rows, tn), lambda l, j: (l, 0, j)),
        out_shape=jax.ShapeDtypeStruct((depth, rows, n), F32),
        compiler_params=_params(("arbitrary", "arbitrary")),
        name="adaln_mod",
    )(cvec, w_mod, b_mod.reshape(depth, 1, n))


def _rms_mod(x, g, sc, sh):
    y = x * lax.rsqrt(jnp.mean(x * x, axis=-1, keepdims=True) + EPS)
    return (y * g) * (1.0 + sc) + sh


def _inproj_kernel(x_ref, g_ref, sc_ref, sh_ref, w_ref, qkv_ref, hy_ref, cf_ref):
    h = _rms_mod(x_ref[...], g_ref[...], sc_ref[0], sh_ref[0]).astype(BF16)
    qkv_ref[:, :D_NA] = (jnp.dot(h, w_ref[:, :D_NA], preferred_element_type=F32) * Q_SCALE).astype(BF16)
    qkv_ref[:, D_NA:2 * D_NA] = jnp.dot(h, w_ref[:, D_NA:2 * D_NA], preferred_element_type=F32).astype(BF16)
    v = jnp.dot(h, w_ref[:, 2 * D_NA:D_QKV], preferred_element_type=F32).astype(BF16)
    lane = lax.broadcasted_iota(I32, (v.shape[0], V_SLOT - NA_HEAD_DIM), 1)
    ones_col = jnp.where(lane == 0, 1.0, 0.0).astype(BF16)
    for hd in range(NA_HEADS):
        lo = 2 * D_NA + hd * V_SLOT
        qkv_ref[:, lo:lo + NA_HEAD_DIM] = v[:, hd * NA_HEAD_DIM:(hd + 1) * NA_HEAD_DIM]
        qkv_ref[:, lo + NA_HEAD_DIM:lo + V_SLOT] = ones_col
    hy_ref[...] = jnp.dot(h, w_ref[:, D_QKV:D_QKV + D_HYIN], preferred_element_type=F32)
    cf_ref[...] = jnp.dot(h, w_ref[:, D_QKV + D_HYIN:], preferred_element_type=F32)


def _per_sample_spec(arr, tiles_per_sample):
    if arr.shape[0] == 1:
        return pl.BlockSpec((1, 1, D_MODEL), lambda i: (0, 0, 0))
    return pl.BlockSpec((1, 1, D_MODEL), lambda i: (i // tiles_per_sample, 0, 0))


def _in_projection(x2d, g, sc, sh, w_bf, seq, tm):
    n = x2d.shape[0]
    tps = seq // tm
    d_in = w_bf.shape[1]
    row = lambda w: pl.BlockSpec((tm, w), lambda i: (i, 0))
    return pl.pallas_call(
        _inproj_kernel,
        grid=(n // tm,),
        in_specs=[row(D_MODEL), pl.BlockSpec((1, D_MODEL), lambda i: (0, 0)),
                  _per_sample_spec(sc, tps), _per_sample_spec(sh, tps),
                  pl.BlockSpec((D_MODEL, d_in), lambda i: (0, 0))],
        out_specs=[row(D_QKVX), row(D_HYIN), row(D_CFIN)],
        out_shape=[jax.ShapeDtypeStruct((n, D_QKVX), BF16), jax.ShapeDtypeStruct((n, D_HYIN), F32),
                   jax.ShapeDtypeStruct((n, D_CFIN), F32)],
        compiler_params=_params(("parallel",)),
        name="in_projection",
    )(x2d, g, sc, sh, w_bf)


def _na_bias_tiles(rpb, rows):
    col = np.arange(GRID_W)
    c0 = np.clip(col - NA_WIN_W // 2, 0, GRID_W - NA_WIN_W)
    col_ok = (col[None, :] >= c0[:, None]) & (col[None, :] < c0[:, None] + NA_WIN_W)
    dc = np.clip(col[None, :] - col[:, None], 1 - NA_WIN_W, NA_WIN_W - 1) + NA_WIN_W - 1
    n_groups = rows // NA_QROWS
    n_dr, n_dc = 2 * NA_WIN_H - 1, 2 * NA_WIN_W - 1
    dr_hot = np.zeros((3, NA_QROWS, NA_KROWS, n_dr), np.float32)
    ok = np.zeros((3, NA_QROWS, GRID_W, NA_KROWS, GRID_W), bool)
    for v, grp in enumerate((0, 1, n_groups - 1)):
        r = grp * NA_QROWS + np.arange(NA_QROWS)
        k0 = np.clip(grp * NA_QROWS - NA_WIN_H // 2, 0, rows - NA_KROWS)
        kr = k0 + np.arange(NA_KROWS)
        r0 = np.clip(r - NA_WIN_H // 2, 0, rows - NA_WIN_H)
        row_ok = (kr[None, :] >= r0[:, None]) & (kr[None, :] < r0[:, None] + NA_WIN_H)
        dr = kr[None, :] - r[:, None] + NA_WIN_H - 1
        for i, j in zip(*np.nonzero(row_ok)):
            dr_hot[v, i, j, dr[i, j]] = 1.0
        ok[v] = row_ok[:, None, :, None] & col_ok[None, :, None, :]
    dc_hot = (dc[None, :, :] == np.arange(n_dc)[:, None, None]).astype(np.float32)
    t1 = jnp.einsum('hrd,dqk->hrqk', rpb.astype(F32), jnp.asarray(dc_hot), precision=HIGHEST)
    bias = jnp.einsum('vijr,hrqk->vhiqjk', jnp.asarray(dr_hot), t1, precision=HIGHEST)
    bias = jnp.where(jnp.asarray(ok)[:, None], bias * math.log2(math.e), NEG_INF)
    return bias.reshape(3, NA_HEADS, NA_QROWS * GRID_W, NA_KROWS * GRID_W)


def _softmax_pv(s_list, v_list):
    m = functools.reduce(jnp.maximum, [jnp.max(s, axis=-1, keepdims=True) for s in s_list])
    o = functools.reduce(jnp.add, [jnp.dot(jnp.exp2(s - m).astype(BF16), v, preferred_element_type=F32)
                                   for s, v in zip(s_list, v_list)])
    return o[:, :NA_HEAD_DIM] / o[:, NA_HEAD_DIM:NA_HEAD_DIM + 1]


def _na_kernel(q_ref, k_ref, v_ref, kc_ref, vc_ref, bias_ref, o_ref, *, rows):
    g = pl.program_id(1)
    k0 = jnp.clip(g * NA_QROWS - NA_WIN_H // 2, 0, rows - NA_KROWS) * GRID_W
    k0 = pl.multiple_of(k0, GRID_W)
    nk = NA_KROWS * GRID_W
    def scores(h):
        sl = slice(h * NA_HEAD_DIM, (h + 1) * NA_HEAD_DIM)
        qh = q_ref[:, sl]
        return [_nt_dot(qh, k_ref[pl.ds(k0, nk), sl]) + bias_ref[0, h], _nt_dot(qh, kc_ref[:, sl])]

    s = scores(0)
    for h in range(NA_HEADS):
        s_next = scores(h + 1) if h + 1 < NA_HEADS else None
        vs = slice(h * V_SLOT, (h + 1) * V_SLOT)
        o = _softmax_pv(s, [v_ref[pl.ds(k0, nk), vs], vc_ref[:, vs]])
        o_ref[:, h * NA_HEAD_DIM:(h + 1) * NA_HEAD_DIM] = o.astype(o_ref.dtype)
        s = s_next


def _neighbourhood_attention(qkv, qkv_c, bias, batch, seq, ctx_len):
    rows = seq // GRID_W
    n_groups = rows // NA_QROWS
    nq, nk = NA_QROWS * GRID_W, NA_KROWS * GRID_W
    sel = lambda g: jnp.where(g == 0, 0, jnp.where(g == n_groups - 1, 2, 1))
    return pl.pallas_call(
        functools.partial(_na_kernel, rows=rows),
        grid=(batch, n_groups),
        in_specs=[pl.BlockSpec((nq, D_NA), lambda b, g: (b * n_groups + g, 0)),
                  pl.BlockSpec((seq, D_NA), lambda b, g: (b, 1)),
                  pl.BlockSpec((seq, NA_HEADS * V_SLOT), lambda b, g: (b, 1)),
                  pl.BlockSpec((ctx_len, D_NA), lambda b, g: (b, 1)),
                  pl.BlockSpec((ctx_len, NA_HEADS * V_SLOT), lambda b, g: (b, 1)),
                  pl.BlockSpec((1, NA_HEADS, nq, nk), lambda b, g: (sel(g), 0, 0, 0))],
        out_specs=pl.BlockSpec((nq, D_NA), lambda b, g: (b * n_groups + g, 0)),
        out_shape=jax.ShapeDtypeStruct((batch * seq, D_NA), BF16),
        compiler_params=_params(("parallel", "arbitrary")),
        name="neighbourhood_attention",
    )(qkv, qkv, qkv, qkv_c, qkv_c, bias)


def _ctx_attn_kernel(q_ref, k_ref, v_ref, o_ref):
    for h in range(NA_HEADS):
        sl = slice(h * NA_HEAD_DIM, (h + 1) * NA_HEAD_DIM)
        o = _softmax_pv([_nt_dot(q_ref[:, sl], k_ref[:, sl])], [v_ref[:, h * V_SLOT:(h + 1) * V_SLOT]])
        o_ref[:, sl] = o.astype(o_ref.dtype)


def _context_attention(qkv_c, batch, ctx_len):
    spec = lambda j: pl.BlockSpec((ctx_len, D_NA), lambda b: (b, j))
    return pl.pallas_call(
        _ctx_attn_kernel,
        grid=(batch,),
        in_specs=[spec(0), spec(1), pl.BlockSpec((ctx_len, NA_HEADS * V_SLOT), lambda b: (b, 1))],
        out_specs=spec(0),
        out_shape=jax.ShapeDtypeStruct((batch * ctx_len, D_NA), BF16),
        compiler_params=_params(("parallel",)),
        name="context_attention",
    )(qkv_c, qkv_c, qkv_c)


SUBLANES = 8
LANES = 128
SHORT_CHUNK = 128
DW_CHUNK = 128
DW_STAGES = 2
HY_HALO = 8
CF_HALO = 16


def _conv_kernel(hy_ref, cf_ref, sw_ref, sb_ref, dw_ref, db_ref, lg_ref, lb_ref,
                 x1_ref, x2_ref, v_ref, cfo_ref, hpad, gpad, ph_scr, *, seq):
    hpad[0:HY_HALO, :] = jnp.zeros((HY_HALO, D_HYIN), F32)
    hpad[seq + HY_HALO:seq + 2 * HY_HALO, :] = jnp.zeros((HY_HALO, D_HYIN), F32)
    hpad[HY_HALO:seq + HY_HALO, :] = hy_ref[...]
    gpad[0:CF_HALO, :] = jnp.zeros((CF_HALO, D_CF), F32)
    gpad[seq + CF_HALO:seq + 2 * CF_HALO, :] = jnp.zeros((CF_HALO, D_CF), F32)
    gpad[CF_HALO:seq + CF_HALO, :] = cf_ref[:, :D_CF] * jax.nn.sigmoid(cf_ref[:, D_CF:])
    pad_s = 1
    pad_c = (CF_KERNEL - 1) // 2

    def short_body(i, carry):
        ch = SHORT_CHUNK
        base = pl.multiple_of(i * ch, ch)
        win = hpad[pl.ds(base, ch + 2 * HY_HALO), :]
        u = sb_ref[...]
        for t in range(3):
            o = HY_HALO - pad_s + t
            u = u + win[o:o + ch, :] * sw_ref[t:t + 1, :]
        x1_ref[i] = u[:, :D_HY].T
        x2_ref[i] = u[:, D_HY:2 * D_HY].T
        v_ref[i] = u[:, 2 * D_HY:].T
        return carry

    lax.fori_loop(0, seq // SHORT_CHUNK, short_body, 0)

    def dw_chunk(base, stage):
        ch = DW_CHUNK
        gw = gpad[pl.ds(base, ch + 2 * CF_HALO), :]
        span = ch + 2 * CF_HALO - SUBLANES
        for phase in range(1, SUBLANES):
            ph_scr[stage, phase - 1] = gw[phase:phase + span, :]
        y = db_ref[...]
        for t in range(CF_KERNEL):
            o = CF_HALO - pad_c + t
            phase = o % SUBLANES
            if phase == 0:
                src = gpad[pl.ds(base + o, ch), :]
            else:
                src = ph_scr[stage, phase - 1, o - phase:o - phase + ch, :]
            y = y + src * dw_ref[t:t + 1, :]
        mu = jnp.mean(y, axis=-1, keepdims=True)
        yc = y - mu
        var = jnp.mean(yc * yc, axis=-1, keepdims=True)
        yn = yc * lax.rsqrt(var + EPS) * lg_ref[...] + lb_ref[...]
        cfo_ref[pl.ds(base, ch), :] = (yn * jax.nn.sigmoid(yn)).astype(cfo_ref.dtype)

    def dw_body(i, carry):
        for stage in range(DW_STAGES):
            dw_chunk(pl.multiple_of((i * DW_STAGES + stage) * DW_CHUNK, DW_CHUNK), stage)
        return carry

    lax.fori_loop(0, seq // (DW_CHUNK * DW_STAGES), dw_body, 0)


def _conv_mixers(hy_in, cf_in, sw, sb, dw, db, lg, lb, batch, seq):
    full = lambda a: pl.BlockSpec(a.shape, lambda b: (0, 0))
    cm = pl.BlockSpec((seq // LANES, None, D_HY, LANES), lambda b: (0, b, 0, 0))
    cm_shape = jax.ShapeDtypeStruct((seq // LANES, batch, D_HY, LANES), F32)
    return pl.pallas_call(
        functools.partial(_conv_kernel, seq=seq),
        grid=(batch,),
        in_specs=[pl.BlockSpec((seq, D_HYIN), lambda b: (b, 0)), pl.BlockSpec((seq, D_CFIN), lambda b: (b, 0)),
                  full(sw), full(sb), full(dw), full(db), full(lg), full(lb)],
        out_specs=[cm, cm, cm, pl.BlockSpec((seq, D_CF), lambda b: (b, 0))],
        out_shape=[cm_shape, cm_shape, cm_shape, jax.ShapeDtypeStruct((batch * seq, D_CF), BF16)],
        scratch_shapes=[pltpu.VMEM((seq + 2 * HY_HALO, D_HYIN), F32), pltpu.VMEM((seq + 2 * CF_HALO, D_CF), F32),
                        pltpu.VMEM((DW_STAGES, SUBLANES - 1, DW_CHUNK + 2 * CF_HALO - SUBLANES, D_CF), F32)],
        compiler_params=_params(("parallel",)),
        name="conv_mixers",
    )(hy_in, cf_in, sw, sb, dw, db, lg, lb)


def _filter_kernel(fb_ref, w1t_ref, w1c_ref, w1s_ref, b1_ref, fr_ref, w2_ref, b2_ref, w3_ref, rate_ref, o_ref, *, seq):
    pos = lax.broadcasted_iota(I32, (seq, 1), 0).astype(F32)
    t = pos / max(seq - 1, 1)
    ang = fb_ref[...] * (2.0 * math.pi) * pos / seq
    dot = functools.partial(jnp.dot, preferred_element_type=F32, precision=HIGHEST)
    fr = fr_ref[...]
    z1 = t * w1t_ref[...] + dot(jnp.cos(ang), w1c_ref[...]) + dot(-jnp.sin(ang), w1s_ref[...]) + b1_ref[...]
    h = jnp.sin(fr * z1)
    h = jnp.sin(fr * (dot(h, w2_ref[...]) + b2_ref[...]))
    o_ref[...] = dot(h, w3_ref[...]) * jnp.exp(-t * rate_ref[...])


def _hyena_filters(seq, w1, b1, freq, w2, b2, w3):
    bands = (HY_EMB - 1) // 2
    fb = jnp.linspace(1e-4, bands - 1, bands, dtype=F32)[None, :]
    min_decay = math.log(HY_TARGET) / HY_SLOW_DECAY
    max_decay = math.log(HY_TARGET) / HY_FAST_DECAY
    rate = jnp.abs(jnp.linspace(min_decay, max_decay, D_HY, dtype=F32))
    rate4 = jnp.tile(rate, 4)[None, :]
    args = (fb, w1[0:1], w1[1:1 + bands], w1[1 + bands:], b1[None, :], freq[None, :], w2, b2[None, :], w3, rate4)
    h = pl.pallas_call(
        functools.partial(_filter_kernel, seq=seq),
        out_shape=jax.ShapeDtypeStruct((seq, 4 * D_HY), F32),
        compiler_params=_params(None),
        name="hyena_filter",
    )(*args)
    h = h.reshape(seq, 2, 2, D_HY)
    fwd, bwd = h[:, 0], h[:, 1]
    line = jnp.concatenate([jnp.zeros_like(fwd[:1]), bwd[1:][::-1], fwd], axis=0)
    return jnp.transpose(line, (1, 2, 0))


HY_CG = 8


def _hyena_kernel(d_ref, x1_ref, x2_ref, v_ref, kf_ref, o_ref, w_scr, acc_scr, *, batch, seq):
    nb = seq // HY_T
    tiles = HY_T // LANES
    cg = pl.program_id(0)

    def rows_of(ref):
        return ref.reshape(seq // LANES * batch * HY_CG, LANES)

    def tile_rows(k, c):
        return pl.ds((k * batch) * HY_CG + c, batch, stride=HY_CG)

    def blocks(ref, c):
        rows = rows_of(ref)
        return jnp.concatenate(
            [jnp.concatenate([rows[tile_rows(j * tiles + h, c), :] for h in range(tiles)], axis=1)
             for j in range(nb)], axis=0)

    out_rows = rows_of(o_ref)
    for c in range(HY_CG):
        z = blocks(v_ref, c)
        for n, gate_ref in enumerate((x1_ref, x2_ref)):
            line = jnp.broadcast_to(kf_ref[n, c:c + 1, :], (HY_T, 2 * seq))
            w_scr[...] = pltpu.roll(line, 0, 1, stride=1, stride_axis=0).astype(BF16)
            zb = z.astype(BF16)
            acc_scr[...] = jnp.dot(zb, w_scr[:, seq:seq + HY_T], preferred_element_type=F32)
            for d in range(1, nb):
                m = batch * (nb - d)
                acc_scr[batch * d:, :] += jnp.dot(zb[:m], w_scr[:, seq + d * HY_T:seq + (d + 1) * HY_T],
                                                  preferred_element_type=F32)
                acc_scr[:m, :] += jnp.dot(zb[batch * d:], w_scr[:, seq - d * HY_T:seq - (d - 1) * HY_T],
                                          preferred_element_type=F32)
            z = blocks(gate_ref, c) * (acc_scr[...] + d_ref[n, cg * HY_CG + c] * z)
        for j in range(nb):
            for h in range(tiles):
                out_rows[tile_rows(j * tiles + h, c), :] = z[j * batch:(j + 1) * batch, h * LANES:(h + 1) * LANES]


def _hyena_long_conv(x1t, x2t, vt, kf, bias_d, batch, seq):
    cm = pl.BlockSpec((seq // LANES, batch, HY_CG, LANES), lambda g: (0, 0, g, 0))
    return pl.pallas_call(
        functools.partial(_hyena_kernel, batch=batch, seq=seq),
        grid=(D_HY // HY_CG,),
        in_specs=[pl.BlockSpec(memory_space=pltpu.SMEM), cm, cm, cm,
                  pl.BlockSpec((2, HY_CG, 2 * seq), lambda g: (0, g, 0))],
        out_specs=cm,
        out_shape=jax.ShapeDtypeStruct((seq // LANES, batch, D_HY, LANES), F32),
        scratch_shapes=[pltpu.VMEM((HY_T, 2 * seq), BF16), pltpu.VMEM((seq // HY_T * batch, HY_T), F32)],
        compiler_params=_params(("parallel",)),
        name="hyena_long_conv",
    )(bias_d, x1t, x2t, vt, kf)


def _outproj_kernel(attn_ref, hy_ref, cf_ref, x_ref, g1_ref, w_ref, ng_ref, sc_ref, sh_ref, rw_ref,
                    xo_ref, h2_ref, lt_ref):
    y = jnp.dot(attn_ref[...], w_ref[:D_NA], preferred_element_type=F32)
    hy = jnp.concatenate([hy_ref[k].T for k in range(hy_ref.shape[0])], axis=0).astype(BF16)
    y = y + jnp.dot(hy, w_ref[D_NA:D_NA + D_HY], preferred_element_type=F32)
    y = y + jnp.dot(cf_ref[...], w_ref[D_NA + D_HY:], preferred_element_type=F32)
    xn = x_ref[...] + g1_ref[0] * y
    xo_ref[...] = xn
    h2 = _rms_mod(xn, ng_ref[...], sc_ref[0], sh_ref[0]).astype(BF16)
    h2_ref[...] = h2
    lt_ref[0] = _nt_dot(rw_ref[...], h2)


def _out_projection(attn, hyt, cf, x2d, g1, w_bf, ng, sc, sh, rw_t, batch, seq, tm):
    n = x2d.shape[0]
    tps = seq // tm
    row = lambda w: pl.BlockSpec((tm, w), lambda i: (i, 0))
    full = lambda a: pl.BlockSpec(a.shape, lambda i: (0, 0))
    return pl.pallas_call(
        _outproj_kernel,
        grid=(n // tm,),
        in_specs=[row(D_NA), pl.BlockSpec((tm // LANES, None, D_HY, LANES), lambda i: (i % tps, i // tps, 0, 0)),
                  row(D_CF), row(D_MODEL),
                  _per_sample_spec(g1, tps), full(w_bf), full(ng), _per_sample_spec(sc, tps),
                  _per_sample_spec(sh, tps), full(rw_t)],
        out_specs=[row(D_MODEL), row(D_MODEL), pl.BlockSpec((1, N_EXPERTS, tm), lambda i: (i // tps, 0, i % tps))],
        out_shape=[jax.ShapeDtypeStruct((n, D_MODEL), F32), jax.ShapeDtypeStruct((n, D_MODEL), BF16),
                   jax.ShapeDtypeStruct((batch, N_EXPERTS, seq), F32)],
        compiler_params=_params(("parallel",)),
        name="out_projection",
    )(attn, hyt, cf, x2d, g1, w_bf, ng, sc, sh, rw_t)


def _route_kernel(lt_ref, tri_ref, pos_ref, aff_ref, *, cap):
    lg = lt_ref[...]
    bs, _, seq = lg.shape
    e = jnp.exp(lg - jnp.max(lg, axis=1, keepdims=True))
    aff = e / jnp.sum(e, axis=1, keepdims=True)
    aff_ref[...] = aff
    keys = pltpu.bitcast(aff.reshape(bs * N_EXPERTS, seq), I32)

    def body(i, thr):
        cand = thr | (jnp.int32(1) << (30 - i))
        cnt = jnp.sum(jnp.where(keys >= cand, 1.0, 0.0), axis=1, keepdims=True)
        return jnp.where(cnt >= cap, cand, thr)

    thr = lax.fori_loop(0, 31, body, jnp.zeros((bs * N_EXPERTS, 1), I32))
    gt = keys > thr
    eq = keys == thr
    need = cap - jnp.sum(jnp.where(gt, 1.0, 0.0), axis=1, keepdims=True)
    rank = jnp.dot(jnp.where(eq, 1.0, 0.0).astype(BF16), tri_ref[...], preferred_element_type=F32)
    sel = gt | (eq & (rank < need))
    slot = jnp.dot(jnp.where(sel, 1.0, 0.0).astype(BF16), tri_ref[...], preferred_element_type=F32)
    pos_ref[...] = jnp.where(sel, slot, -1.0).astype(I32).reshape(bs, N_EXPERTS, seq)


def _routing(logits_t, tri, cap):
    batch, _, seq = logits_t.shape
    bs = 8 if batch % 8 == 0 else batch
    blk = pl.BlockSpec((bs, N_EXPERTS, seq), lambda i: (i, 0, 0))
    return pl.pallas_call(
        functools.partial(_route_kernel, cap=cap),
        grid=(batch // bs,),
        in_specs=[blk, pl.BlockSpec((seq, seq), lambda i: (0, 0))],
        out_specs=[blk, blk],
        out_shape=[jax.ShapeDtypeStruct((batch, N_EXPERTS, seq), I32),
                   jax.ShapeDtypeStruct((batch, N_EXPERTS, seq), F32)],
        compiler_params=_params(("parallel",)),
        name="ec_routing",
    )(logits_t, tri)


def _gather_kernel(pos_ref, h_ref, xe_ref, *, cap):
    seq = h_ref.shape[0]
    slot = lax.broadcasted_iota(I32, (cap, seq), 0)
    for e in range(N_EXPERTS):
        onehot = jnp.where(pos_ref[0, e:e + 1, :] == slot, 1.0, 0.0).astype(BF16)
        xe_ref[0, e * cap:(e + 1) * cap, :] = jnp.dot(onehot, h_ref[...], preferred_element_type=F32).astype(BF16)


def _gather_tokens(pos, h2, cap):
    batch, _, seq = pos.shape
    return pl.pallas_call(
        functools.partial(_gather_kernel, cap=cap),
        grid=(batch,),
        in_specs=[pl.BlockSpec((1, N_EXPERTS, seq), lambda b: (b, 0, 0)),
                  pl.BlockSpec((seq, D_MODEL), lambda b: (b, 0))],
        out_specs=pl.BlockSpec((1, N_EXPERTS * cap, D_MODEL), lambda b: (b, 0, 0)),
        out_shape=jax.ShapeDtypeStruct((batch, N_EXPERTS * cap, D_MODEL), BF16),
        compiler_params=_params(("parallel",)),
        name="ec_gather",
    )(pos, h2)


FFN_ROWS = 512


def _ffn_kernel(*refs, layer, n_ctx_tiles):
    if n_ctx_tiles:
        xe_ref, xc_ref, w1_hbm, w3_hbm, w2_hbm, y_ref, yc_ref, w1b, w3b, w2b, st1, st3, st2, sem = refs
    else:
        xe_ref, w1_hbm, w3_hbm, w2_hbm, y_ref, w1b, w3b, w2b, st1, st3, st2, sem = refs
    e, rt = pl.program_id(0), pl.program_id(1)
    n_e, n_rt = N_EXPERTS, pl.num_programs(1)
    r1, r2 = st1.shape[1], st2.shape[1]

    def slab_copies(expert, k, slot):
        o1, o2 = pl.multiple_of(k * r1, r1), pl.multiple_of(k * r2, r2)
        return (pltpu.make_async_copy(w1_hbm.at[layer, expert, pl.ds(o1, r1), :], st1.at[slot], sem.at[slot, 0]),
                pltpu.make_async_copy(w3_hbm.at[layer, expert, pl.ds(o1, r1), :], st3.at[slot], sem.at[slot, 1]),
                pltpu.make_async_copy(w2_hbm.at[layer, expert, pl.ds(o2, r2), :], st2.at[slot], sem.at[slot, 2]))

    def start_slab(expert, k, slot):
        for cp in slab_copies(expert, k, slot):
            cp.start()

    def cast_slab(k, slot, wslot):
        o1, o2 = pl.multiple_of(k * r1, r1), pl.multiple_of(k * r2, r2)
        w1b[wslot, pl.ds(o1, r1), :] = st1[slot].astype(BF16)
        w3b[wslot, pl.ds(o1, r1), :] = st3[slot].astype(BF16)
        w2b[wslot, pl.ds(o2, r2), :] = st2[slot].astype(BF16)

    def land_slab(expert, k, slot, wslot):
        for cp in slab_copies(expert, k, slot):
            cp.wait()
        cast_slab(k, slot, wslot)

    def prefetched_expert(expert):
        return jnp.minimum(expert + 1, n_e - 1)

    n_slabs = D_MODEL // r1

    @pl.when((e == 0) & (rt == 0))
    def _():
        start_slab(0, 0, 0)
        for k in range(n_slabs):
            if k + 1 < n_slabs:
                start_slab(0, k + 1, (k + 1) % 2)
            land_slab(0, k, k % 2, 0)
        start_slab(prefetched_expert(0), 0, 0)

    slot = (e * n_rt + rt) % 2
    for cp in slab_copies(prefetched_expert(e), rt, slot):
        cp.wait()
    last_tile = rt + 1 == n_rt
    nxt_e = jnp.minimum(jnp.where(last_tile, e + 1, e), n_e - 1)
    nxt_k = jnp.where(last_tile, 0, rt + 1)
    start_slab(prefetched_expert(nxt_e), nxt_k, 1 - slot)
    cast_slab(rt, slot, (e + 1) % 2)
    ws = e % 2

    def swiglu(x_ref, o_ref):
        bt, cap, _ = x_ref.shape
        x = x_ref[...].reshape(bt * cap, D_MODEL)
        a = jnp.dot(x, w1b[ws], preferred_element_type=F32)
        u = jnp.dot(x, w3b[ws], preferred_element_type=F32)
        h = (a * jax.nn.sigmoid(a) * u).astype(BF16)
        o_ref[...] = jnp.dot(h, w2b[ws], preferred_element_type=F32).astype(BF16).reshape(bt, cap, D_MODEL)

    swiglu(xe_ref, y_ref)
    if n_ctx_tiles:
        @pl.when(rt < n_ctx_tiles)
        def _():
            swiglu(xc_ref, yc_ref)

    @pl.when((e == n_e - 1) & last_tile)
    def _():
        for cp in slab_copies(prefetched_expert(nxt_e), nxt_k, 1 - slot):
            cp.wait()


def _tile_samples(batch, cap):
    bt = max(1, min(batch, FFN_ROWS // cap))
    while batch % bt:
        bt -= 1
    return bt


def _expert_ffn(xe, xe_ctx, w1, w3, w2, layer):
    batch = xe.shape[0]
    cap = xe.shape[1] // N_EXPERTS
    bt = _tile_samples(batch, cap)
    n_rt = batch // bt
    xspec = pl.BlockSpec((bt, cap, D_MODEL), lambda e, r: (r, e, 0))
    hbm = pl.BlockSpec(memory_space=pl.ANY)
    in_specs, out_specs, out_shape, args, n_ctx_tiles = [xspec], [xspec], [jax.ShapeDtypeStruct(xe.shape, BF16)], [xe], 0
    if xe_ctx is not None:
        cap_c = xe_ctx.shape[1] // N_EXPERTS
        bt_c = _tile_samples(batch, cap_c)
        n_ctx_tiles = batch // bt_c
        assert n_ctx_tiles <= n_rt
        cspec = pl.BlockSpec((bt_c, cap_c, D_MODEL), lambda e, r: (jnp.minimum(r, n_ctx_tiles - 1), e, 0))
        in_specs.append(cspec)
        out_specs.append(cspec)
        out_shape.append(jax.ShapeDtypeStruct(xe_ctx.shape, BF16))
        args.append(xe_ctx)
    assert D_MODEL % n_rt == 0 and EXPERT_FF % n_rt == 0
    r1, r2 = D_MODEL // n_rt, EXPERT_FF // n_rt
    out = pl.pallas_call(
        functools.partial(_ffn_kernel, layer=layer, n_ctx_tiles=n_ctx_tiles),
        grid=(N_EXPERTS, n_rt),
        in_specs=in_specs + [hbm, hbm, hbm],
        out_specs=out_specs,
        out_shape=out_shape,
        scratch_shapes=[pltpu.VMEM((2, D_MODEL, EXPERT_FF), BF16), pltpu.VMEM((2, D_MODEL, EXPERT_FF), BF16),
                        pltpu.VMEM((2, EXPERT_FF, D_MODEL), BF16), pltpu.VMEM((2, r1, EXPERT_FF), F32),
                        pltpu.VMEM((2, r1, EXPERT_FF), F32), pltpu.VMEM((2, r2, D_MODEL), F32),
                        pltpu.SemaphoreType.DMA((2, 3))],
        compiler_params=_params(("arbitrary", "arbitrary")),
        name="ec_expert_ffn",
    )(*args, w1, w3, w2)
    return (out[0], out[1]) if xe_ctx is not None else (out[0], None)


def _scatter_kernel(pos_ref, aff_ref, y_ref, x_ref, g2_ref, fg_ref, o_ref, pg_scr, *, cap, final_norm):
    tl = x_ref.shape[0]
    slot = lax.broadcasted_iota(I32, (cap, tl), 0)
    for e in range(N_EXPERTS):
        hit = pos_ref[0, e:e + 1, :] == slot
        pg_scr[e * cap:(e + 1) * cap, :] = jnp.where(hit, aff_ref[0, e:e + 1, :], 0.0).astype(BF16)
    moe = _tn_dot(pg_scr[...], y_ref[0])
    xn = x_ref[...] + g2_ref[0] * moe
    if final_norm:
        xn = xn * lax.rsqrt(jnp.mean(xn * xn, axis=-1, keepdims=True) + EPS) * fg_ref[...]
    o_ref[...] = xn


def _scatter_residual(pos, aff, y, x2d, g2, fg, cap, tl, final_norm):
    batch, _, seq = pos.shape
    tps = seq // tl
    rspec = pl.BlockSpec((1, N_EXPERTS, tl), lambda b, t: (b, 0, t))
    xspec = pl.BlockSpec((tl, D_MODEL), lambda b, t: (b * tps + t, 0))
    g2spec = (pl.BlockSpec((1, 1, D_MODEL), lambda b, t: (0, 0, 0)) if g2.shape[0] == 1
              else pl.BlockSpec((1, 1, D_MODEL), lambda b, t: (b, 0, 0)))
    return pl.pallas_call(
        functools.partial(_scatter_kernel, cap=cap, final_norm=final_norm),
        grid=(batch, tps),
        in_specs=[rspec, rspec, pl.BlockSpec((1, N_EXPERTS * cap, D_MODEL), lambda b, t: (b, 0, 0)), xspec,
                  g2spec, pl.BlockSpec((1, D_MODEL), lambda b, t: (0, 0))],
        out_specs=xspec,
        out_shape=jax.ShapeDtypeStruct(x2d.shape, F32),
        scratch_shapes=[pltpu.VMEM((N_EXPERTS * cap, tl), BF16)],
        compiler_params=_params(("parallel", "arbitrary")),
        name="ec_scatter_residual",
    )(pos, aff, y, x2d, g2, fg)


def _strict_lower_ones(n):
    i = np.arange(n)
    return jnp.asarray(i[:, None] < i[None, :], dtype=BF16)


def _stream_front(x2d, mods, lw, batch, seq, tm, attn_fn):
    sh1, sc1, g1, sh2, sc2, g2 = mods
    qkv, hy_in, cf_in = _in_projection(x2d, lw["norm1_g"], sc1, sh1, lw["w_in"], seq, tm)
    attn = attn_fn(qkv)
    x1t, x2t, vt, cf = _conv_mixers(hy_in, cf_in, lw["hy_short_w"], lw["hy_short_b"], lw["cf_dw_w"], lw["cf_dw_b"],
                                    lw["cf_ln_g"], lw["cf_ln_b"], batch, seq)
    kf = _hyena_filters(seq, lw["hy_filt_w1"], lw["hy_filt_b1"], lw["hy_filt_freq"], lw["hy_filt_w2"],
                        lw["hy_filt_b2"], lw["hy_filt_w3"])
    hyt = _hyena_long_conv(x1t, x2t, vt, kf, lw["hy_bias_d"], batch, seq)
    x_mid, h2, logits_t = _out_projection(attn, hyt, cf, x2d, g1, lw["w_out"], lw["norm2_g"], sc2, sh2,
                                          lw["router_wt"], batch, seq, tm)
    cap = EC_CAPACITY * seq // N_EXPERTS
    pos, aff = _routing(logits_t, _strict_lower_ones(seq), cap)
    xe = _gather_tokens(pos, h2, cap)
    return dict(pos=pos, aff=aff, xe=xe, x_mid=x_mid, g2=g2, cap=cap, seq=seq, qkv=qkv)


def _stream_back(front, y, lw, final_norm):
    return _scatter_residual(front["pos"], front["aff"], y, front["x_mid"], front["g2"], lw["final_norm_g"],
                             front["cap"], min(front["seq"], 512), final_norm)


def kernel(x, c, ctx, c_ctx, w_mod, b_mod, norm1_g, norm2_g, w_in, na_rpb, hy_short_w, hy_short_b, hy_filt_w1,
           hy_filt_b1, hy_filt_freq, hy_filt_w2, hy_filt_b2, hy_filt_w3, hy_bias_d, cf_dw_w, cf_dw_b, cf_ln_g,
           cf_ln_b, w_out, router_w, expert_w1, expert_w3, expert_w2, final_norm_g):
    batch, seq, _ = x.shape
    ctx_len = ctx.shape[1]
    depth = w_mod.shape[0]
    rows = seq // GRID_W

    n_c = batch + 1
    n_c_pad = -(-n_c // 8) * 8
    cvec = jnp.concatenate([c, c_ctx[None, :], jnp.zeros((n_c_pad - n_c, D_MODEL), F32)], axis=0)
    mod_all = _modulation(cvec, w_mod, b_mod)

    xl = x.reshape(batch * seq, D_MODEL)
    xc = ctx.reshape(batch * ctx_len, D_MODEL)
    for l in range(depth):
        last = l == depth - 1
        lw = dict(
            norm1_g=norm1_g[l][None, :], norm2_g=norm2_g[l][None, :], w_in=w_in[l].astype(BF16),
            hy_short_w=hy_short_w[l], hy_short_b=hy_short_b[l][None, :], hy_filt_w1=hy_filt_w1[l],
            hy_filt_b1=hy_filt_b1[l], hy_filt_freq=hy_filt_freq[l], hy_filt_w2=hy_filt_w2[l],
            hy_filt_b2=hy_filt_b2[l], hy_filt_w3=hy_filt_w3[l], hy_bias_d=hy_bias_d[l], cf_dw_w=cf_dw_w[l],
            cf_dw_b=cf_dw_b[l][None, :], cf_ln_g=cf_ln_g[l][None, :], cf_ln_b=cf_ln_b[l][None, :],
            w_out=w_out[l].astype(BF16), router_wt=router_w[l].T.astype(BF16),
            expert_w1=expert_w1, expert_w3=expert_w3, expert_w2=expert_w2, layer=l,
            final_norm_g=final_norm_g[None, :])
        chunks = [mod_all[l, :, j * D_MODEL:(j + 1) * D_MODEL] for j in range(N_MOD)]
        mods_l = [m[:batch, None, :] for m in chunks]
        mods_c = [m[batch:batch + 1, None, :] for m in chunks]
        bias = _na_bias_tiles(na_rpb[l], rows)

        if last:
            front_c = None
            qkv_c, _, _ = _in_projection(xc, lw["norm1_g"], mods_c[1], mods_c[0], lw["w_in"], ctx_len, ctx_len)
        else:
            front_c = _stream_front(xc, mods_c, lw, batch, ctx_len, ctx_len,
                                    lambda qkv: _context_attention(qkv, batch, ctx_len))
            qkv_c = front_c["qkv"]
        front_l = _stream_front(xl, mods_l, lw, batch, seq, 512,
                                lambda qkv: _neighbourhood_attention(qkv, qkv_c, bias, batch, seq, ctx_len))
        y_l, y_c = _expert_ffn(front_l["xe"], None if last else front_c["xe"], expert_w1, expert_w3, expert_w2, l)
        if not last:
            xc = _stream_back(front_c, y_c, lw, False)
        xl = _stream_back(front_l, y_l, lw, last)
    return xl.reshape(batch, seq, D_MODEL)
```

```python
import functools
import math

import numpy as np
import jax
import jax.numpy as jnp
from jax import lax
from jax.experimental import pallas as pl
from jax.experimental.pallas import tpu as pltpu

F32, BF16, I32 = jnp.float32, jnp.bfloat16, jnp.int32
HIGHEST = lax.Precision.HIGHEST

D_MODEL = 1024
GRID_W = 64
N_MOD = 6
EPS = 1e-6
NEG_INF = -1e30
NA_HEAD_DIM = 64
D_NA = 512
NA_HEADS = 8
NA_WIN_H = 8
NA_WIN_W = 16
D_HY = 256
HY_EMB = 33
HY_FAST_DECAY = 0.3
HY_SLOW_DECAY = 1.5
HY_TARGET = 1e-2
D_CF = 256
CF_KERNEL = 31
D_QKV = 3 * D_NA
V_SLOT = 128
D_QKVX = 2 * D_NA + NA_HEADS * V_SLOT
Q_SCALE = NA_HEAD_DIM ** -0.5 * math.log2(math.e)
D_HYIN = 3 * D_HY
D_CFIN = 2 * D_CF
N_EXPERTS = 16
EC_CAPACITY = 2
EXPERT_FF = 2 * D_MODEL

NA_QROWS = 4
NA_KROWS = NA_QROWS + NA_WIN_H - 1
HY_T = 256
MIB = 1024 * 1024
VMEM_LIMIT = 56 * MIB


def _params(sem, vmem=VMEM_LIMIT):
    return pltpu.CompilerParams(dimension_semantics=sem, vmem_limit_bytes=vmem)


def _nt_dot(a, b):
    return lax.dot_general(a, b, (((1,), (1,)), ((), ())), preferred_element_type=F32)


def _tn_dot(a, b):
    return lax.dot_general(a, b, (((0,), (0,)), ((), ())), preferred_element_type=F32)


def _mod_kernel(c_ref, w_ref, b_ref, o_ref):
    c = c_ref[...]
    s = c * jax.nn.sigmoid(c)
    o_ref[0] = jnp.dot(s, w_ref[0], preferred_element_type=F32, precision=HIGHEST) + b_ref[0]


def _modulation(cvec, w_mod, b_mod):
    depth, _, n = w_mod.shape
    rows = cvec.shape[0]
    tn = 1536
    return pl.pallas_call(
        _mod_kernel,
        grid=(depth, n // tn),
        in_specs=[pl.BlockSpec((rows, D_MODEL), lambda l, j: (0, 0)),
                  pl.BlockSpec((1, D_MODEL, tn), lambda l, j: (l, 0, j)),
                  pl.BlockSpec((1, 1, tn), lambda l, j: (l, 0, j))],
        out_specs=pl.BlockSpec((1, rows, tn), lambda l, j: (l, 0, j)),
        out_shape=jax.ShapeDtypeStruct((depth, rows, n), F32),
        compiler_params=_params(("arbitrary", "arbitrary")),
        name="adaln_mod",
    )(cvec, w_mod, b_mod.reshape(depth, 1, n))


def _rms_mod(x, g, sc, sh):
    y = x * lax.rsqrt(jnp.mean(x * x, axis=-1, keepdims=True) + EPS)
    return (y * g) * (1.0 + sc) + sh


def _inproj_kernel(x_ref, g_ref, sc_ref, sh_ref, w_ref, qkv_ref, hy_ref, cf_ref):
    h = _rms_mod(x_ref[...], g_ref[...], sc_ref[0], sh_ref[0]).astype(BF16)
    qkv_ref[:, :D_NA] = (jnp.dot(h, w_ref[:, :D_NA], preferred_element_type=F32) * Q_SCALE).astype(BF16)
    qkv_ref[:, D_NA:2 * D_NA] = jnp.dot(h, w_ref[:, D_NA:2 * D_NA], preferred_element_type=F32).astype(BF16)
    v = jnp.dot(h, w_ref[:, 2 * D_NA:D_QKV], preferred_element_type=F32).astype(BF16)
    lane = lax.broadcasted_iota(I32, (v.shape[0], V_SLOT - NA_HEAD_DIM), 1)
    ones_col = jnp.where(lane == 0, 1.0, 0.0).astype(BF16)
    for hd in range(NA_HEADS):
        lo = 2 * D_NA + hd * V_SLOT
        qkv_ref[:, lo:lo + NA_HEAD_DIM] = v[:, hd * NA_HEAD_DIM:(hd + 1) * NA_HEAD_DIM]
        qkv_ref[:, lo + NA_HEAD_DIM:lo + V_SLOT] = ones_col
    hy_ref[...] = jnp.dot(h, w_ref[:, D_QKV:D_QKV + D_HYIN], preferred_element_type=F32)
    cf_ref[...] = jnp.dot(h, w_ref[:, D_QKV + D_HYIN:], preferred_element_type=F32)


def _per_sample_spec(arr, tiles_per_sample):
    if arr.shape[0] == 1:
        return pl.BlockSpec((1, 1, D_MODEL), lambda i: (0, 0, 0))
    return pl.BlockSpec((1, 1, D_MODEL), lambda i: (i // tiles_per_sample, 0, 0))


def _in_projection(x2d, g, sc, sh, w_bf, seq, tm):
    n = x2d.shape[0]
    tps = seq // tm
    d_in = w_bf.shape[1]
    row = lambda w: pl.BlockSpec((tm, w), lambda i: (i, 0))
    return pl.pallas_call(
        _inproj_kernel,
        grid=(n // tm,),
        in_specs=[row(D_MODEL), pl.BlockSpec((1, D_MODEL), lambda i: (0, 0)),
                  _per_sample_spec(sc, tps), _per_sample_spec(sh, tps),
                  pl.BlockSpec((D_MODEL, d_in), lambda i: (0, 0))],
        out_specs=[row(D_QKVX), row(D_HYIN), row(D_CFIN)],
        out_shape=[jax.ShapeDtypeStruct((n, D_QKVX), BF16), jax.ShapeDtypeStruct((n, D_HYIN), F32),
                   jax.ShapeDtypeStruct((n, D_CFIN), F32)],
        compiler_params=_params(("parallel",)),
        name="in_projection",
    )(x2d, g, sc, sh, w_bf)


def _na_bias_tiles(rpb, rows):
    col = np.arange(GRID_W)
    c0 = np.clip(col - NA_WIN_W // 2, 0, GRID_W - NA_WIN_W)
    col_ok = (col[None, :] >= c0[:, None]) & (col[None, :] < c0[:, None] + NA_WIN_W)
    dc = np.clip(col[None, :] - col[:, None], 1 - NA_WIN_W, NA_WIN_W - 1) + NA_WIN_W - 1
    n_groups = rows // NA_QROWS
    n_dr, n_dc = 2 * NA_WIN_H - 1, 2 * NA_WIN_W - 1
    dr_hot = np.zeros((3, NA_QROWS, NA_KROWS, n_dr), np.float32)
    ok = np.zeros((3, NA_QROWS, GRID_W, NA_KROWS, GRID_W), bool)
    for v, grp in enumerate((0, 1, n_groups - 1)):
        r = grp * NA_QROWS + np.arange(NA_QROWS)
        k0 = np.clip(grp * NA_QROWS - NA_WIN_H // 2, 0, rows - NA_KROWS)
        kr = k0 + np.arange(NA_KROWS)
        r0 = np.clip(r - NA_WIN_H // 2, 0, rows - NA_WIN_H)
        row_ok = (kr[None, :] >= r0[:, None]) & (kr[None, :] < r0[:, None] + NA_WIN_H)
        dr = kr[None, :] - r[:, None] + NA_WIN_H - 1
        for i, j in zip(*np.nonzero(row_ok)):
            dr_hot[v, i, j, dr[i, j]] = 1.0
        ok[v] = row_ok[:, None, :, None] & col_ok[None, :, None, :]
    dc_hot = (dc[None, :, :] == np.arange(n_dc)[:, None, None]).astype(np.float32)
    t1 = jnp.einsum('hrd,dqk->hrqk', rpb.astype(F32), jnp.asarray(dc_hot), precision=HIGHEST)
    bias = jnp.einsum('vijr,hrqk->vhiqjk', jnp.asarray(dr_hot), t1, precision=HIGHEST)
    bias = jnp.where(jnp.asarray(ok)[:, None], bias * math.log2(math.e), NEG_INF)
    return bias.reshape(3, NA_HEADS, NA_QROWS * GRID_W, NA_KROWS * GRID_W)


def _softmax_pv(s_list, v_list):
    m = functools.reduce(jnp.maximum, [jnp.max(s, axis=-1, keepdims=True) for s in s_list])
    o = functools.reduce(jnp.add, [jnp.dot(jnp.exp2(s - m).astype(BF16), v, preferred_element_type=F32)
                                   for s, v in zip(s_list, v_list)])
    return o[:, :NA_HEAD_DIM] / o[:, NA_HEAD_DIM:NA_HEAD_DIM + 1]


def _na_kernel(q_ref, k_ref, v_ref, kc_ref, vc_ref, bias_ref, o_ref, *, rows):
    g = pl.program_id(1)
    k0 = jnp.clip(g * NA_QROWS - NA_WIN_H // 2, 0, rows - NA_KROWS) * GRID_W
    k0 = pl.multiple_of(k0, GRID_W)
    nk = NA_KROWS * GRID_W
    def scores(h):
        sl = slice(h * NA_HEAD_DIM, (h + 1) * NA_HEAD_DIM)
        qh = q_ref[:, sl]
        return [_nt_dot(qh, k_ref[pl.ds(k0, nk), sl]) + bias_ref[0, h], _nt_dot(qh, kc_ref[:, sl])]

    s = scores(0)
    for h in range(NA_HEADS):
        s_next = scores(h + 1) if h + 1 < NA_HEADS else None
        vs = slice(h * V_SLOT, (h + 1) * V_SLOT)
        o = _softmax_pv(s, [v_ref[pl.ds(k0, nk), vs], vc_ref[:, vs]])
        o_ref[:, h * NA_HEAD_DIM:(h + 1) * NA_HEAD_DIM] = o.astype(o_ref.dtype)
        s = s_next


def _neighbourhood_attention(qkv, qkv_c, bias, batch, seq, ctx_len):
    rows = seq // GRID_W
    n_groups = rows // NA_QROWS
    nq, nk = NA_QROWS * GRID_W, NA_KROWS * GRID_W
    sel = lambda g: jnp.where(g == 0, 0, jnp.where(g == n_groups - 1, 2, 1))
    return pl.pallas_call(
        functools.partial(_na_kernel, rows=rows),
        grid=(batch, n_groups),
        in_specs=[pl.BlockSpec((nq, D_NA), lambda b, g: (b * n_groups + g, 0)),
                  pl.BlockSpec((seq, D_NA), lambda b, g: (b, 1)),
                  pl.BlockSpec((seq, NA_HEADS * V_SLOT), lambda b, g: (b, 1)),
                  pl.BlockSpec((ctx_len, D_NA), lambda b, g: (b, 1)),
                  pl.BlockSpec((ctx_len, NA_HEADS * V_SLOT), lambda b, g: (b, 1)),
                  pl.BlockSpec((1, NA_HEADS, nq, nk), lambda b, g: (sel(g), 0, 0, 0))],
        out_specs=pl.BlockSpec((nq, D_NA), lambda b, g: (b * n_groups + g, 0)),
        out_shape=jax.ShapeDtypeStruct((batch * seq, D_NA), BF16),
        compiler_params=_params(("parallel", "arbitrary")),
        name="neighbourhood_attention",
    )(qkv, qkv, qkv, qkv_c, qkv_c, bias)


def _ctx_attn_kernel(q_ref, k_ref, v_ref, o_ref):
    for h in range(NA_HEADS):
        sl = slice(h * NA_HEAD_DIM, (h + 1) * NA_HEAD_DIM)
        o = _softmax_pv([_nt_dot(q_ref[:, sl], k_ref[:, sl])], [v_ref[:, h * V_SLOT:(h + 1) * V_SLOT]])
        o_ref[:, sl] = o.astype(o_ref.dtype)


def _context_attention(qkv_c, batch, ctx_len):
    spec = lambda j: pl.BlockSpec((ctx_len, D_NA), lambda b: (b, j))
    return pl.pallas_call(
        _ctx_attn_kernel,
        grid=(batch,),
        in_specs=[spec(0), spec(1), pl.BlockSpec((ctx_len, NA_HEADS * V_SLOT), lambda b: (b, 1))],
        out_specs=spec(0),
        out_shape=jax.ShapeDtypeStruct((batch * ctx_len, D_NA), BF16),
        compiler_params=_params(("parallel",)),
        name="context_attention",
    )(qkv_c, qkv_c, qkv_c)


SUBLANES = 8
LANES = 128
SHORT_CHUNK = 128
DW_CHUNK = 128
DW_STAGES = 2
HY_HALO = 8
CF_HALO = 16


def _conv_kernel(hy_ref, cf_ref, sw_ref, sb_ref, dw_ref, db_ref, lg_ref, lb_ref,
                 x1_ref, x2_ref, v_ref, cfo_ref, hpad, gpad, ph_scr, *, seq):
    hpad[0:HY_HALO, :] = jnp.zeros((HY_HALO, D_HYIN), F32)
    hpad[seq + HY_HALO:seq + 2 * HY_HALO, :] = jnp.zeros((HY_HALO, D_HYIN), F32)
    hpad[HY_HALO:seq + HY_HALO, :] = hy_ref[...]
    gpad[0:CF_HALO, :] = jnp.zeros((CF_HALO, D_CF), F32)
    gpad[seq + CF_HALO:seq + 2 * CF_HALO, :] = jnp.zeros((CF_HALO, D_CF), F32)
    gpad[CF_HALO:seq + CF_HALO, :] = cf_ref[:, :D_CF] * jax.nn.sigmoid(cf_ref[:, D_CF:])
    pad_s = 1
    pad_c = (CF_KERNEL - 1) // 2

    def short_body(i, carry):
        ch = SHORT_CHUNK
        base = pl.multiple_of(i * ch, ch)
        win = hpad[pl.ds(base, ch + 2 * HY_HALO), :]
        u = sb_ref[...]
        for t in range(3):
            o = HY_HALO - pad_s + t
            u = u + win[o:o + ch, :] * sw_ref[t:t + 1, :]
        x1_ref[i] = u[:, :D_HY].T
        x2_ref[i] = u[:, D_HY:2 * D_HY].T
        v_ref[i] = u[:, 2 * D_HY:].T
        return carry

    lax.fori_loop(0, seq // SHORT_CHUNK, short_body, 0)

    def dw_chunk(base, stage):
        ch = DW_CHUNK
        gw = gpad[pl.ds(base, ch + 2 * CF_HALO), :]
        span = ch + 2 * CF_HALO - SUBLANES
        for phase in range(1, SUBLANES):
            ph_scr[stage, phase - 1] = gw[phase:phase + span, :]
        y = db_ref[...]
        for t in range(CF_KERNEL):
            o = CF_HALO - pad_c + t
            phase = o % SUBLANES
            if phase == 0:
                src = gpad[pl.ds(base + o, ch), :]
            else:
                src = ph_scr[stage, phase - 1, o - phase:o - phase + ch, :]
            y = y + src * dw_ref[t:t + 1, :]
        mu = jnp.mean(y, axis=-1, keepdims=True)
        yc = y - mu
        var = jnp.mean(yc * yc, axis=-1, keepdims=True)
        yn = yc * lax.rsqrt(var + EPS) * lg_ref[...] + lb_ref[...]
        cfo_ref[pl.ds(base, ch), :] = (yn * jax.nn.sigmoid(yn)).astype(cfo_ref.dtype)

    def dw_body(i, carry):
        for stage in range(DW_STAGES):
            dw_chunk(pl.multiple_of((i * DW_STAGES + stage) * DW_CHUNK, DW_CHUNK), stage)
        return carry

    lax.fori_loop(0, seq // (DW_CHUNK * DW_STAGES), dw_body, 0)


def _conv_mixers(hy_in, cf_in, sw, sb, dw, db, lg, lb, batch, seq):
    full = lambda a: pl.BlockSpec(a.shape, lambda b: (0, 0))
    cm = pl.BlockSpec((seq // LANES, None, D_HY, LANES), lambda b: (0, b, 0, 0))
    cm_shape = jax.ShapeDtypeStruct((seq // LANES, batch, D_HY, LANES), F32)
    return pl.pallas_call(
        functools.partial(_conv_kernel, seq=seq),
        grid=(batch,),
        in_specs=[pl.BlockSpec((seq, D_HYIN), lambda b: (b, 0)), pl.BlockSpec((seq, D_CFIN), lambda b: (b, 0)),
                  full(sw), full(sb), full(dw), full(db), full(lg), full(lb)],
        out_specs=[cm, cm, cm, pl.BlockSpec((seq, D_CF), lambda b: (b, 0))],
        out_shape=[cm_shape, cm_shape, cm_shape, jax.ShapeDtypeStruct((batch * seq, D_CF), BF16)],
        scratch_shapes=[pltpu.VMEM((seq + 2 * HY_HALO, D_HYIN), F32), pltpu.VMEM((seq + 2 * CF_HALO, D_CF), F32),
                        pltpu.VMEM((DW_STAGES, SUBLANES - 1, DW_CHUNK + 2 * CF_HALO - SUBLANES, D_CF), F32)],
        compiler_params=_params(("parallel",)),
        name="conv_mixers",
    )(hy_in, cf_in, sw, sb, dw, db, lg, lb)


def _filter_kernel(fb_ref, w1t_ref, w1c_ref, w1s_ref, b1_ref, fr_ref, w2_ref, b2_ref, w3_ref, rate_ref, o_ref, *, seq):
    pos = lax.broadcasted_iota(I32, (seq, 1), 0).astype(F32)
    t = pos / max(seq - 1, 1)
    ang = fb_ref[...] * (2.0 * math.pi) * pos / seq
    dot = functools.partial(jnp.dot, preferred_element_type=F32, precision=HIGHEST)
    fr = fr_ref[...]
    z1 = t * w1t_ref[...] + dot(jnp.cos(ang), w1c_ref[...]) + dot(-jnp.sin(ang), w1s_ref[...]) + b1_ref[...]
    h = jnp.sin(fr * z1)
    h = jnp.sin(fr * (dot(h, w2_ref[...]) + b2_ref[...]))
    o_ref[...] = dot(h, w3_ref[...]) * jnp.exp(-t * rate_ref[...])


def _hyena_filters(seq, w1, b1, freq, w2, b2, w3):
    bands = (HY_EMB - 1) // 2
    fb = jnp.linspace(1e-4, bands - 1, bands, dtype=F32)[None, :]
    min_decay = math.log(HY_TARGET) / HY_SLOW_DECAY
    max_decay = math.log(HY_TARGET) / HY_FAST_DECAY
    rate = jnp.abs(jnp.linspace(min_decay, max_decay, D_HY, dtype=F32))
    rate4 = jnp.tile(rate, 4)[None, :]
    args = (fb, w1[0:1], w1[1:1 + bands], w1[1 + bands:], b1[None, :], freq[None, :], w2, b2[None, :], w3, rate4)
    h = pl.pallas_call(
        functools.partial(_filter_kernel, seq=seq),
        out_shape=jax.ShapeDtypeStruct((seq, 4 * D_HY), F32),
        compiler_params=_params(None),
        name="hyena_filter",
    )(*args)
    h = h.reshape(seq, 2, 2, D_HY)
    fwd, bwd = h[:, 0], h[:, 1]
    line = jnp.concatenate([jnp.zeros_like(fwd[:1]), bwd[1:][::-1], fwd], axis=0)
    return jnp.transpose(line, (1, 2, 0))


HY_CG = 8


def _hyena_kernel(d_ref, x1_ref, x2_ref, v_ref, kf_ref, o_ref, w_scr, acc_scr, *, batch, seq):
    nb = seq // HY_T
    tiles = HY_T // LANES
    cg = pl.program_id(0)

    def rows_of(ref):
        return ref.reshape(seq // LANES * batch * HY_CG, LANES)

    def tile_rows(k, c):
        return pl.ds((k * batch) * HY_CG + c, batch, stride=HY_CG)

    def blocks(ref, c):
        rows = rows_of(ref)
        return jnp.concatenate(
            [jnp.concatenate([rows[tile_rows(j * tiles + h, c), :] for h in range(tiles)], axis=1)
             for j in range(nb)], axis=0)

    out_rows = rows_of(o_ref)
    for c in range(HY_CG):
        z = blocks(v_ref, c)
        for n, gate_ref in enumerate((x1_ref, x2_ref)):
            line = jnp.broadcast_to(kf_ref[n, c:c + 1, :], (HY_T, 2 * seq))
            w_scr[...] = pltpu.roll(line, 0, 1, stride=1, stride_axis=0).astype(BF16)
            zb = z.astype(BF16)
            acc_scr[...] = jnp.dot(zb, w_scr[:, seq:seq + HY_T], preferred_element_type=F32)
            for d in range(1, nb):
                m = batch * (nb - d)
                acc_scr[batch * d:, :] += jnp.dot(zb[:m], w_scr[:, seq + d * HY_T:seq + (d + 1) * HY_T],
                                                  preferred_element_type=F32)
                acc_scr[:m, :] += jnp.dot(zb[batch * d:], w_scr[:, seq - d * HY_T:seq - (d - 1) * HY_T],
                                          preferred_element_type=F32)
            z = blocks(gate_ref, c) * (acc_scr[...] + d_ref[n, cg * HY_CG + c] * z)
        for j in range(nb):
            for h in range(tiles):
                out_rows[tile_rows(j * tiles + h, c), :] = z[j * batch:(j + 1) * batch, h * LANES:(h + 1) * LANES]


def _hyena_long_conv(x1t, x2t, vt, kf, bias_d, batch, seq):
    cm = pl.BlockSpec((seq // LANES, batch, HY_CG, LANES), lambda g: (0, 0, g, 0))
    return pl.pallas_call(
        functools.partial(_hyena_kernel, batch=batch, seq=seq),
        grid=(D_HY // HY_CG,),
        in_specs=[pl.BlockSpec(memory_space=pltpu.SMEM), cm, cm, cm,
                  pl.BlockSpec((2, HY_CG, 2 * seq), lambda g: (0, g, 0))],
        out_specs=cm,
        out_shape=jax.ShapeDtypeStruct((seq // LANES, batch, D_HY, LANES), F32),
        scratch_shapes=[pltpu.VMEM((HY_T, 2 * seq), BF16), pltpu.VMEM((seq // HY_T * batch, HY_T), F32)],
        compiler_params=_params(("parallel",)),
        name="hyena_long_conv",
    )(bias_d, x1t, x2t, vt, kf)


def _outproj_kernel(attn_ref, hy_ref, cf_ref, x_ref, g1_ref, w_ref, ng_ref, sc_ref, sh_ref, rw_ref,
                    xo_ref, h2_ref, lt_ref):
    y = jnp.dot(attn_ref[...], w_ref[:D_NA], preferred_element_type=F32)
    hy = jnp.concatenate([hy_ref[k].T for k in range(hy_ref.shape[0])], axis=0).astype(BF16)
    y = y + jnp.dot(hy, w_ref[D_NA:D_NA + D_HY], preferred_element_type=F32)
    y = y + jnp.dot(cf_ref[...], w_ref[D_NA + D_HY:], preferred_element_type=F32)
    xn = x_ref[...] + g1_ref[0] * y
    xo_ref[...] = xn
    h2 = _rms_mod(xn, ng_ref[...], sc_ref[0], sh_ref[0]).astype(BF16)
    h2_ref[...] = h2
    lt_ref[0] = _nt_dot(rw_ref[...], h2)


def _out_projection(attn, hyt, cf, x2d, g1, w_bf, ng, sc, sh, rw_t, batch, seq, tm):
    n = x2d.shape[0]
    tps = seq // tm
    row = lambda w: pl.BlockSpec((tm, w), lambda i: (i, 0))
    full = lambda a: pl.BlockSpec(a.shape, lambda i: (0, 0))
    return pl.pallas_call(
        _outproj_kernel,
        grid=(n // tm,),
        in_specs=[row(D_NA), pl.BlockSpec((tm // LANES, None, D_HY, LANES), lambda i: (i % tps, i // tps, 0, 0)),
                  row(D_CF), row(D_MODEL),
                  _per_sample_spec(g1, tps), full(w_bf), full(ng), _per_sample_spec(sc, tps),
                  _per_sample_spec(sh, tps), full(rw_t)],
        out_specs=[row(D_MODEL), row(D_MODEL), pl.BlockSpec((1, N_EXPERTS, tm), lambda i: (i // tps, 0, i % tps))],
        out_shape=[jax.ShapeDtypeStruct((n, D_MODEL), F32), jax.ShapeDtypeStruct((n, D_MODEL), BF16),
                   jax.ShapeDtypeStruct((batch, N_EXPERTS, seq), F32)],
        compiler_params=_params(("parallel",)),
        name="out_projection",
    )(attn, hyt, cf, x2d, g1, w_bf, ng, sc, sh, rw_t)


def _route_kernel(lt_ref, tri_ref, tcol_ref, pos_ref, aff_ref, cnt_ref, *, cap):
    lg = lt_ref[...]
    bs, _, seq = lg.shape
    e = jnp.exp(lg - jnp.max(lg, axis=1, keepdims=True))
    aff = e / jnp.sum(e, axis=1, keepdims=True)
    aff_ref[...] = aff
    keys = pltpu.bitcast(aff.reshape(bs * N_EXPERTS, seq), I32)

    def body(i, thr):
        cand = thr | (jnp.int32(1) << (30 - i))
        cnt = jnp.sum(jnp.where(keys >= cand, 1.0, 0.0), axis=1, keepdims=True)
        return jnp.where(cnt >= cap, cand, thr)

    thr = lax.fori_loop(0, 31, body, jnp.zeros((bs * N_EXPERTS, 1), I32))
    gt = keys > thr
    eq = keys == thr
    need = cap - jnp.sum(jnp.where(gt, 1.0, 0.0), axis=1, keepdims=True)
    rank = jnp.dot(jnp.where(eq, 1.0, 0.0).astype(BF16), tri_ref[...], preferred_element_type=F32)
    sel = gt | (eq & (rank < need))
    sel_bf = jnp.where(sel, 1.0, 0.0).astype(BF16)
    slot = jnp.dot(sel_bf, tri_ref[...], preferred_element_type=F32)
    pos_ref[...] = jnp.where(sel, slot, -1.0).astype(I32).reshape(bs, N_EXPERTS, seq)
    cnt = jnp.dot(sel_bf, tcol_ref[...], preferred_element_type=F32)
    cnt_ref[...] = cnt.astype(I32).reshape(bs, N_EXPERTS, LANES)


def _routing(logits_t, cap, tl):
    batch, _, seq = logits_t.shape
    bs = 8 if batch % 8 == 0 else batch
    idx = np.arange(seq)
    tri = jnp.asarray(idx[:, None] < idx[None, :], dtype=BF16)
    tile_start = np.minimum(np.arange(LANES) * tl, seq)
    tcol = jnp.asarray(idx[:, None] < tile_start[None, :], dtype=BF16)
    blk = pl.BlockSpec((bs, N_EXPERTS, seq), lambda i: (i, 0, 0))
    cblk = pl.BlockSpec((bs, N_EXPERTS, LANES), lambda i: (i, 0, 0))
    pos, aff, cnt = pl.pallas_call(
        functools.partial(_route_kernel, cap=cap),
        grid=(batch // bs,),
        in_specs=[blk, pl.BlockSpec((seq, seq), lambda i: (0, 0)), pl.BlockSpec((seq, LANES), lambda i: (0, 0))],
        out_specs=[blk, blk, cblk],
        out_shape=[jax.ShapeDtypeStruct((batch, N_EXPERTS, seq), I32),
                   jax.ShapeDtypeStruct((batch, N_EXPERTS, seq), F32),
                   jax.ShapeDtypeStruct((batch, N_EXPERTS, LANES), I32)],
        compiler_params=_params(("parallel",)),
        name="ec_routing",
    )(logits_t, tri, tcol)
    return pos, aff, cnt[:, :, :seq // tl]


def _gather_kernel(pos_ref, h_ref, xe_ref, *, cap):
    seq = h_ref.shape[0]
    slot = lax.broadcasted_iota(I32, (cap, seq), 0)
    for e in range(N_EXPERTS):
        onehot = jnp.where(pos_ref[0, e:e + 1, :] == slot, 1.0, 0.0).astype(BF16)
        xe_ref[0, e * cap:(e + 1) * cap, :] = jnp.dot(onehot, h_ref[...], preferred_element_type=F32).astype(BF16)


def _gather_tokens(pos, h2, cap):
    batch, _, seq = pos.shape
    return pl.pallas_call(
        functools.partial(_gather_kernel, cap=cap),
        grid=(batch,),
        in_specs=[pl.BlockSpec((1, N_EXPERTS, seq), lambda b: (b, 0, 0)),
                  pl.BlockSpec((seq, D_MODEL), lambda b: (b, 0))],
        out_specs=pl.BlockSpec((1, N_EXPERTS * cap, D_MODEL), lambda b: (b, 0, 0)),
        out_shape=jax.ShapeDtypeStruct((batch, N_EXPERTS * cap, D_MODEL), BF16),
        compiler_params=_params(("parallel",)),
        name="ec_gather",
    )(pos, h2)


FFN_ROWS = 512


def _ffn_kernel(*refs, layer, n_ctx_tiles):
    if n_ctx_tiles:
        xe_ref, xc_ref, w1_hbm, w3_hbm, w2_hbm, y_ref, yc_ref, w1b, w3b, w2b, st1, st3, st2, sem = refs
    else:
        xe_ref, w1_hbm, w3_hbm, w2_hbm, y_ref, w1b, w3b, w2b, st1, st3, st2, sem = refs
    e, rt = pl.program_id(0), pl.program_id(1)
    n_e, n_rt = N_EXPERTS, pl.num_programs(1)
    r1, r2 = st1.shape[1], st2.shape[1]

    def slab_copies(expert, k, slot):
        o1, o2 = pl.multiple_of(k * r1, r1), pl.multiple_of(k * r2, r2)
        return (pltpu.make_async_copy(w1_hbm.at[layer, expert, pl.ds(o1, r1), :], st1.at[slot], sem.at[slot, 0]),
                pltpu.make_async_copy(w3_hbm.at[layer, expert, pl.ds(o1, r1), :], st3.at[slot], sem.at[slot, 1]),
                pltpu.make_async_copy(w2_hbm.at[layer, expert, pl.ds(o2, r2), :], st2.at[slot], sem.at[slot, 2]))

    def start_slab(expert, k, slot):
        for cp in slab_copies(expert, k, slot):
            cp.start()

    def cast_slab(k, slot, wslot):
        o1, o2 = pl.multiple_of(k * r1, r1), pl.multiple_of(k * r2, r2)
        w1b[wslot, pl.ds(o1, r1), :] = st1[slot].astype(BF16)
        w3b[wslot, pl.ds(o1, r1), :] = st3[slot].astype(BF16)
        w2b[wslot, pl.ds(o2, r2), :] = st2[slot].astype(BF16)

    def land_slab(expert, k, slot, wslot):
        for cp in slab_copies(expert, k, slot):
            cp.wait()
        cast_slab(k, slot, wslot)

    def prefetched_expert(expert):
        return jnp.minimum(expert + 1, n_e - 1)

    n_slabs = D_MODEL // r1

    @pl.when((e == 0) & (rt == 0))
    def _():
        start_slab(0, 0, 0)
        for k in range(n_slabs):
            if k + 1 < n_slabs:
                start_slab(0, k + 1, (k + 1) % 2)
            land_slab(0, k, k % 2, 0)
        start_slab(prefetched_expert(0), 0, 0)

    slot = (e * n_rt + rt) % 2
    for cp in slab_copies(prefetched_expert(e), rt, slot):
        cp.wait()
    last_tile = rt + 1 == n_rt
    nxt_e = jnp.minimum(jnp.where(last_tile, e + 1, e), n_e - 1)
    nxt_k = jnp.where(last_tile, 0, rt + 1)
    start_slab(prefetched_expert(nxt_e), nxt_k, 1 - slot)
    cast_slab(rt, slot, (e + 1) % 2)
    ws = e % 2

    def swiglu(x_ref, o_ref):
        bt, cap, _ = x_ref.shape
        x = x_ref[...].reshape(bt * cap, D_MODEL)
        a = jnp.dot(x, w1b[ws], preferred_element_type=F32)
        u = jnp.dot(x, w3b[ws], preferred_element_type=F32)
        h = (a * jax.nn.sigmoid(a) * u).astype(BF16)
        o_ref[...] = jnp.dot(h, w2b[ws], preferred_element_type=F32).astype(BF16).reshape(bt, cap, D_MODEL)

    swiglu(xe_ref, y_ref)
    if n_ctx_tiles:
        @pl.when(rt < n_ctx_tiles)
        def _():
            swiglu(xc_ref, yc_ref)

    @pl.when((e == n_e - 1) & last_tile)
    def _():
        for cp in slab_copies(prefetched_expert(nxt_e), nxt_k, 1 - slot):
            cp.wait()


def _tile_samples(batch, cap):
    bt = max(1, min(batch, FFN_ROWS // cap))
    while batch % bt:
        bt -= 1
    return bt


def _expert_ffn(xe, xe_ctx, w1, w3, w2, layer):
    batch = xe.shape[0]
    cap = xe.shape[1] // N_EXPERTS
    bt = _tile_samples(batch, cap)
    n_rt = batch // bt
    xspec = pl.BlockSpec((bt, cap, D_MODEL), lambda e, r: (r, e, 0))
    hbm = pl.BlockSpec(memory_space=pl.ANY)
    in_specs, out_specs, out_shape, args, n_ctx_tiles = [xspec], [xspec], [jax.ShapeDtypeStruct(xe.shape, BF16)], [xe], 0
    if xe_ctx is not None:
        cap_c = xe_ctx.shape[1] // N_EXPERTS
        bt_c = _tile_samples(batch, cap_c)
        n_ctx_tiles = batch // bt_c
        assert n_ctx_tiles <= n_rt
        cspec = pl.BlockSpec((bt_c, cap_c, D_MODEL), lambda e, r: (jnp.minimum(r, n_ctx_tiles - 1), e, 0))
        in_specs.append(cspec)
        out_specs.append(cspec)
        out_shape.append(jax.ShapeDtypeStruct(xe_ctx.shape, BF16))
        args.append(xe_ctx)
    assert D_MODEL % n_rt == 0 and EXPERT_FF % n_rt == 0
    r1, r2 = D_MODEL // n_rt, EXPERT_FF // n_rt
    out = pl.pallas_call(
        functools.partial(_ffn_kernel, layer=layer, n_ctx_tiles=n_ctx_tiles),
        grid=(N_EXPERTS, n_rt),
        in_specs=in_specs + [hbm, hbm, hbm],
        out_specs=out_specs,
        out_shape=out_shape,
        scratch_shapes=[pltpu.VMEM((2, D_MODEL, EXPERT_FF), BF16), pltpu.VMEM((2, D_MODEL, EXPERT_FF), BF16),
                        pltpu.VMEM((2, EXPERT_FF, D_MODEL), BF16), pltpu.VMEM((2, r1, EXPERT_FF), F32),
                        pltpu.VMEM((2, r1, EXPERT_FF), F32), pltpu.VMEM((2, r2, D_MODEL), F32),
                        pltpu.SemaphoreType.DMA((2, 3))],
        compiler_params=_params(("arbitrary", "arbitrary")),
        name="ec_expert_ffn",
    )(*args, w1, w3, w2)
    return (out[0], out[1]) if xe_ctx is not None else (out[0], None)


SCATTER_TILE = 512
SLOT_ALIGN = 16


def _scatter_kernel(cnt_ref, pos_ref, aff_ref, y_ref, x_ref, g2_ref, fg_ref, o_ref, pg_scr, yw_scr, *,
                    cap, window, final_norm):
    b, t = pl.program_id(0), pl.program_id(1)
    n_t = pl.num_programs(1)
    tl = x_ref.shape[0]

    def finish(moe):
        xn = x_ref[...] + g2_ref[0] * moe
        if final_norm:
            xn = xn * lax.rsqrt(jnp.mean(xn * xn, axis=-1, keepdims=True) + EPS) * fg_ref[...]
        o_ref[...] = xn

    def scatter_all_slots():
        slot = lax.broadcasted_iota(I32, (cap, tl), 0)
        for e in range(N_EXPERTS):
            hit = pos_ref[0, e:e + 1, :] == slot
            pg_scr[e * cap:(e + 1) * cap, :] = jnp.where(hit, aff_ref[0, e:e + 1, :], 0.0).astype(BF16)
        finish(_tn_dot(pg_scr[...], y_ref[0]))

    if window == cap:
        scatter_all_slots()
        return

    starts, fits = [], True
    for e in range(N_EXPERTS):
        base = (b * N_EXPERTS + e) * n_t + t
        lo = cnt_ref[base]
        hi = jnp.where(t + 1 < n_t, cnt_ref[jnp.minimum(base + 1, cnt_ref.shape[0] - 1)], cap)
        start = jnp.minimum(lo // SLOT_ALIGN * SLOT_ALIGN, cap - window)
        starts.append(pl.multiple_of(start, SLOT_ALIGN))
        fits = jnp.logical_and(fits, hi - start <= window)

    @pl.when(fits)
    def _():
        slot = lax.broadcasted_iota(I32, (window, tl), 0)
        for e in range(N_EXPERTS):
            hit = pos_ref[0, e:e + 1, :] == slot + starts[e]
            pg_scr[e * window:(e + 1) * window, :] = jnp.where(hit, aff_ref[0, e:e + 1, :], 0.0).astype(BF16)
            row0 = pl.multiple_of(e * cap + starts[e], SLOT_ALIGN)
            yw_scr[e * window:(e + 1) * window, :] = y_ref[0, pl.ds(row0, window), :]
        finish(_tn_dot(pg_scr[:N_EXPERTS * window, :], yw_scr[...]))

    @pl.when(jnp.logical_not(fits))
    def _():
        scatter_all_slots()


def _scatter_residual(pos, aff, cnt, y, x2d, g2, fg, cap, tl, final_norm):
    batch, _, seq = pos.shape
    tps = seq // tl
    window = cap // 2 if tps > 1 else cap
    rspec = pl.BlockSpec((1, N_EXPERTS, tl), lambda b, t, c: (b, 0, t))
    xspec = pl.BlockSpec((tl, D_MODEL), lambda b, t, c: (b * tps + t, 0))
    g2spec = (pl.BlockSpec((1, 1, D_MODEL), lambda b, t, c: (0, 0, 0)) if g2.shape[0] == 1
              else pl.BlockSpec((1, 1, D_MODEL), lambda b, t, c: (b, 0, 0)))
    grid_spec = pltpu.PrefetchScalarGridSpec(
        num_scalar_prefetch=1,
        grid=(batch, tps),
        in_specs=[rspec, rspec, pl.BlockSpec((1, N_EXPERTS * cap, D_MODEL), lambda b, t, c: (b, 0, 0)), xspec,
                  g2spec, pl.BlockSpec((1, D_MODEL), lambda b, t, c: (0, 0))],
        out_specs=xspec,
        scratch_shapes=[pltpu.VMEM((N_EXPERTS * cap, tl), BF16), pltpu.VMEM((N_EXPERTS * window, D_MODEL), BF16)])
    return pl.pallas_call(
        functools.partial(_scatter_kernel, cap=cap, window=window, final_norm=final_norm),
        grid_spec=grid_spec,
        out_shape=jax.ShapeDtypeStruct(x2d.shape, F32),
        compiler_params=_params(("parallel", "arbitrary")),
        name="ec_scatter_residual",
    )(cnt.reshape(-1), pos, aff, y, x2d, g2, fg)


def _stream_front(x2d, mods, lw, batch, seq, tm, attn_fn):
    sh1, sc1, g1, sh2, sc2, g2 = mods
    qkv, hy_in, cf_in = _in_projection(x2d, lw["norm1_g"], sc1, sh1, lw["w_in"], seq, tm)
    attn = attn_fn(qkv)
    x1t, x2t, vt, cf = _conv_mixers(hy_in, cf_in, lw["hy_short_w"], lw["hy_short_b"], lw["cf_dw_w"], lw["cf_dw_b"],
                                    lw["cf_ln_g"], lw["cf_ln_b"], batch, seq)
    kf = _hyena_filters(seq, lw["hy_filt_w1"], lw["hy_filt_b1"], lw["hy_filt_freq"], lw["hy_filt_w2"],
                        lw["hy_filt_b2"], lw["hy_filt_w3"])
    hyt = _hyena_long_conv(x1t, x2t, vt, kf, lw["hy_bias_d"], batch, seq)
    x_mid, h2, logits_t = _out_projection(attn, hyt, cf, x2d, g1, lw["w_out"], lw["norm2_g"], sc2, sh2,
                                          lw["router_wt"], batch, seq, tm)
    cap = EC_CAPACITY * seq // N_EXPERTS
    pos, aff, cnt = _routing(logits_t, cap, min(seq, SCATTER_TILE))
    xe = _gather_tokens(pos, h2, cap)
    return dict(pos=pos, aff=aff, cnt=cnt, xe=xe, x_mid=x_mid, g2=g2, cap=cap, seq=seq, qkv=qkv)


def _stream_back(front, y, lw, final_norm):
    return _scatter_residual(front["pos"], front["aff"], front["cnt"], y, front["x_mid"], front["g2"],
                             lw["final_norm_g"], front["cap"], min(front["seq"], SCATTER_TILE), final_norm)


def kernel(x, c, ctx, c_ctx, w_mod, b_mod, norm1_g, norm2_g, w_in, na_rpb, hy_short_w, hy_short_b, hy_filt_w1,
           hy_filt_b1, hy_filt_freq, hy_filt_w2, hy_filt_b2, hy_filt_w3, hy_bias_d, cf_dw_w, cf_dw_b, cf_ln_g,
           cf_ln_b, w_out, router_w, expert_w1, expert_w3, expert_w2, final_norm_g):
    batch, seq, _ = x.shape
    ctx_len = ctx.shape[1]
    depth = w_mod.shape[0]
    rows = seq // GRID_W

    n_c = batch + 1
    n_c_pad = -(-n_c // 8) * 8
    cvec = jnp.concatenate([c, c_ctx[None, :], jnp.zeros((n_c_pad - n_c, D_MODEL), F32)], axis=0)
    mod_all = _modulation(cvec, w_mod, b_mod)

    xl = x.reshape(batch * seq, D_MODEL)
    xc = ctx.reshape(batch * ctx_len, D_MODEL)
    for l in range(depth):
        last = l == depth - 1
        lw = dict(
            norm1_g=norm1_g[l][None, :], norm2_g=norm2_g[l][None, :], w_in=w_in[l].astype(BF16),
            hy_short_w=hy_short_w[l], hy_short_b=hy_short_b[l][None, :], hy_filt_w1=hy_filt_w1[l],
            hy_filt_b1=hy_filt_b1[l], hy_filt_freq=hy_filt_freq[l], hy_filt_w2=hy_filt_w2[l],
            hy_filt_b2=hy_filt_b2[l], hy_filt_w3=hy_filt_w3[l], hy_bias_d=hy_bias_d[l], cf_dw_w=cf_dw_w[l],
            cf_dw_b=cf_dw_b[l][None, :], cf_ln_g=cf_ln_g[l][None, :], cf_ln_b=cf_ln_b[l][None, :],
            w_out=w_out[l].astype(BF16), router_wt=router_w[l].T.astype(BF16),
            expert_w1=expert_w1, expert_w3=expert_w3, expert_w2=expert_w2, layer=l,
            final_norm_g=final_norm_g[None, :])
        chunks = [mod_all[l, :, j * D_MODEL:(j + 1) * D_MODEL] for j in range(N_MOD)]
        mods_l = [m[:batch, None, :] for m in chunks]
        mods_c = [m[batch:batch + 1, None, :] for m in chunks]
        bias = _na_bias_tiles(na_rpb[l], rows)

        if last:
            front_c = None
            qkv_c, _, _ = _in_projection(xc, lw["norm1_g"], mods_c[1], mods_c[0], lw["w_in"], ctx_len, ctx_len)
        else:
            front_c = _stream_front(xc, mods_c, lw, batch, ctx_len, ctx_len,
                                    lambda qkv: _context_attention(qkv, batch, ctx_len))
            qkv_c = front_c["qkv"]
        front_l = _stream_front(xl, mods_l, lw, batch, seq, 512,
                                lambda qkv: _neighbourhood_attention(qkv, qkv_c, bias, batch, seq, ctx_len))
        y_l, y_c = _expert_ffn(front_l["xe"], None if last else front_c["xe"], expert_w1, expert_w3, expert_w2, l)
        if not last:
            xc = _stream_back(front_c, y_c, lw, False)
        xl = _stream_back(front_l, y_l, lw, last)
    return xl.reshape(batch, seq, D_MODEL)
```

```python
import functools
import math

import numpy as np
import jax
import jax.numpy as jnp
from jax import lax
from jax.experimental import pallas as pl
from jax.experimental.pallas import tpu as pltpu

F32, BF16, I32 = jnp.float32, jnp.bfloat16, jnp.int32
HIGHEST = lax.Precision.HIGHEST

D_MODEL = 1024
GRID_W = 64
N_MOD = 6
EPS = 1e-6
NEG_INF = -1e30
NA_HEAD_DIM = 64
D_NA = 512
NA_HEADS = 8
NA_WIN_H = 8
NA_WIN_W = 16
D_HY = 256
HY_EMB = 33
HY_FAST_DECAY = 0.3
HY_SLOW_DECAY = 1.5
HY_TARGET = 1e-2
D_CF = 256
CF_KERNEL = 31
D_QKV = 3 * D_NA
V_SLOT = 128
D_QKVX = 2 * D_NA + NA_HEADS * V_SLOT
Q_SCALE = NA_HEAD_DIM ** -0.5 * math.log2(math.e)
D_HYIN = 3 * D_HY
D_CFIN = 2 * D_CF
N_EXPERTS = 16
EC_CAPACITY = 2
EXPERT_FF = 2 * D_MODEL

NA_QROWS = 4
NA_KROWS = NA_QROWS + NA_WIN_H - 1
HY_T = 256
MIB = 1024 * 1024
VMEM_LIMIT = 56 * MIB


def _params(sem, vmem=VMEM_LIMIT):
    return pltpu.CompilerParams(dimension_semantics=sem, vmem_limit_bytes=vmem)


def _nt_dot(a, b):
    return lax.dot_general(a, b, (((1,), (1,)), ((), ())), preferred_element_type=F32)


def _tn_dot(a, b):
    return lax.dot_general(a, b, (((0,), (0,)), ((), ())), preferred_element_type=F32)


def _mod_kernel(c_ref, w_ref, b_ref, o_ref):
    c = c_ref[...]
    s = c * jax.nn.sigmoid(c)
    o_ref[0] = jnp.dot(s, w_ref[0], preferred_element_type=F32, precision=HIGHEST) + b_ref[0]


def _modulation(cvec, w_mod, b_mod):
    depth, _, n = w_mod.shape
    rows = cvec.shape[0]
    tn = 1536
    return pl.pallas_call(
        _mod_kernel,
        grid=(depth, n // tn),
        in_specs=[pl.BlockSpec((rows, D_MODEL), lambda l, j: (0, 0)),
                  pl.BlockSpec((1, D_MODEL, tn), lambda l, j: (l, 0, j)),
                  pl.BlockSpec((1, 1, tn), lambda l, j: (l, 0, j))],
        out_specs=pl.BlockSpec((1, rows, tn), lambda l, j: (l, 0, j)),
        out_shape=jax.ShapeDtypeStruct((depth, rows, n), F32),
        compiler_params=_params(("arbitrary", "arbitrary")),
        name="adaln_mod",
    )(cvec, w_mod, b_mod.reshape(depth, 1, n))


def _rms_mod(x, g, sc, sh):
    y = x * lax.rsqrt(jnp.mean(x * x, axis=-1, keepdims=True) + EPS)
    return (y * g) * (1.0 + sc) + sh


def _inproj_kernel(x_ref, g_ref, sc_ref, sh_ref, w_ref, qkv_ref, hy_ref, cf_ref):
    h = _rms_mod(x_ref[...], g_ref[...], sc_ref[0], sh_ref[0]).astype(BF16)
    qkv_ref[:, :D_NA] = (jnp.dot(h, w_ref[:, :D_NA], preferred_element_type=F32) * Q_SCALE).astype(BF16)
    qkv_ref[:, D_NA:2 * D_NA] = jnp.dot(h, w_ref[:, D_NA:2 * D_NA], preferred_element_type=F32).astype(BF16)
    v = jnp.dot(h, w_ref[:, 2 * D_NA:D_QKV], preferred_element_type=F32).astype(BF16)
    lane = lax.broadcasted_iota(I32, (v.shape[0], V_SLOT - NA_HEAD_DIM), 1)
    ones_col = jnp.where(lane == 0, 1.0, 0.0).astype(BF16)
    for hd in range(NA_HEADS):
        lo = 2 * D_NA + hd * V_SLOT
        qkv_ref[:, lo:lo + NA_HEAD_DIM] = v[:, hd * NA_HEAD_DIM:(hd + 1) * NA_HEAD_DIM]
        qkv_ref[:, lo + NA_HEAD_DIM:lo + V_SLOT] = ones_col
    hy_ref[...] = jnp.dot(h, w_ref[:, D_QKV:D_QKV + D_HYIN], preferred_element_type=F32)
    cf_ref[...] = jnp.dot(h, w_ref[:, D_QKV + D_HYIN:], preferred_element_type=F32)


def _per_sample_spec(arr, tiles_per_sample):
    if arr.shape[0] == 1:
        return pl.BlockSpec((1, 1, D_MODEL), lambda i: (0, 0, 0))
    return pl.BlockSpec((1, 1, D_MODEL), lambda i: (i // tiles_per_sample, 0, 0))


def _in_projection(x2d, g, sc, sh, w_bf, seq, tm):
    n = x2d.shape[0]
    tps = seq // tm
    d_in = w_bf.shape[1]
    row = lambda w: pl.BlockSpec((tm, w), lambda i: (i, 0))
    return pl.pallas_call(
        _inproj_kernel,
        grid=(n // tm,),
        in_specs=[row(D_MODEL), pl.BlockSpec((1, D_MODEL), lambda i: (0, 0)),
                  _per_sample_spec(sc, tps), _per_sample_spec(sh, tps),
                  pl.BlockSpec((D_MODEL, d_in), lambda i: (0, 0))],
        out_specs=[row(D_QKVX), row(D_HYIN), row(D_CFIN)],
        out_shape=[jax.ShapeDtypeStruct((n, D_QKVX), BF16), jax.ShapeDtypeStruct((n, D_HYIN), F32),
                   jax.ShapeDtypeStruct((n, D_CFIN), F32)],
        compiler_params=_params(("parallel",)),
        name="in_projection",
    )(x2d, g, sc, sh, w_bf)


def _na_bias_tiles(rpb, rows):
    col = np.arange(GRID_W)
    c0 = np.clip(col - NA_WIN_W // 2, 0, GRID_W - NA_WIN_W)
    col_ok = (col[None, :] >= c0[:, None]) & (col[None, :] < c0[:, None] + NA_WIN_W)
    dc = np.clip(col[None, :] - col[:, None], 1 - NA_WIN_W, NA_WIN_W - 1) + NA_WIN_W - 1
    n_groups = rows // NA_QROWS
    n_dr, n_dc = 2 * NA_WIN_H - 1, 2 * NA_WIN_W - 1
    dr_hot = np.zeros((3, NA_QROWS, NA_KROWS, n_dr), np.float32)
    ok = np.zeros((3, NA_QROWS, GRID_W, NA_KROWS, GRID_W), bool)
    for v, grp in enumerate((0, 1, n_groups - 1)):
        r = grp * NA_QROWS + np.arange(NA_QROWS)
        k0 = np.clip(grp * NA_QROWS - NA_WIN_H // 2, 0, rows - NA_KROWS)
        kr = k0 + np.arange(NA_KROWS)
        r0 = np.clip(r - NA_WIN_H // 2, 0, rows - NA_WIN_H)
        row_ok = (kr[None, :] >= r0[:, None]) & (kr[None, :] < r0[:, None] + NA_WIN_H)
        dr = kr[None, :] - r[:, None] + NA_WIN_H - 1
        for i, j in zip(*np.nonzero(row_ok)):
            dr_hot[v, i, j, dr[i, j]] = 1.0
        ok[v] = row_ok[:, None, :, None] & col_ok[None, :, None, :]
    dc_hot = (dc[None, :, :] == np.arange(n_dc)[:, None, None]).astype(np.float32)
    t1 = jnp.einsum('hrd,dqk->hrqk', rpb.astype(F32), jnp.asarray(dc_hot), precision=HIGHEST)
    bias = jnp.einsum('vijr,hrqk->vhiqjk', jnp.asarray(dr_hot), t1, precision=HIGHEST)
    bias = jnp.where(jnp.asarray(ok)[:, None], bias * math.log2(math.e), NEG_INF)
    return bias.reshape(3, NA_HEADS, NA_QROWS * GRID_W, NA_KROWS * GRID_W)


def _softmax_pv(s_list, v_list):
    m = functools.reduce(jnp.maximum, [jnp.max(s, axis=-1, keepdims=True) for s in s_list])
    o = functools.reduce(jnp.add, [jnp.dot(jnp.exp2(s - m).astype(BF16), v, preferred_element_type=F32)
                                   for s, v in zip(s_list, v_list)])
    return o[:, :NA_HEAD_DIM] / o[:, NA_HEAD_DIM:NA_HEAD_DIM + 1]


def _na_kernel(q_ref, k_ref, v_ref, kc_ref, vc_ref, bias_ref, o_ref, *, rows):
    g = pl.program_id(1)
    k0 = jnp.clip(g * NA_QROWS - NA_WIN_H // 2, 0, rows - NA_KROWS) * GRID_W
    k0 = pl.multiple_of(k0, GRID_W)
    nk = NA_KROWS * GRID_W
    def scores(h):
        sl = slice(h * NA_HEAD_DIM, (h + 1) * NA_HEAD_DIM)
        qh = q_ref[:, sl]
        return [_nt_dot(qh, k_ref[pl.ds(k0, nk), sl]) + bias_ref[0, h], _nt_dot(qh, kc_ref[:, sl])]

    s = scores(0)
    for h in range(NA_HEADS):
        s_next = scores(h + 1) if h + 1 < NA_HEADS else None
        vs = slice(h * V_SLOT, (h + 1) * V_SLOT)
        o = _softmax_pv(s, [v_ref[pl.ds(k0, nk), vs], vc_ref[:, vs]])
        o_ref[:, h * NA_HEAD_DIM:(h + 1) * NA_HEAD_DIM] = o.astype(o_ref.dtype)
        s = s_next


def _neighbourhood_attention(qkv, qkv_c, bias, batch, seq, ctx_len):
    rows = seq // GRID_W
    n_groups = rows // NA_QROWS
    nq, nk = NA_QROWS * GRID_W, NA_KROWS * GRID_W
    sel = lambda g: jnp.where(g == 0, 0, jnp.where(g == n_groups - 1, 2, 1))
    return pl.pallas_call(
        functools.partial(_na_kernel, rows=rows),
        grid=(batch, n_groups),
        in_specs=[pl.BlockSpec((nq, D_NA), lambda b, g: (b * n_groups + g, 0)),
                  pl.BlockSpec((seq, D_NA), lambda b, g: (b, 1)),
                  pl.BlockSpec((seq, NA_HEADS * V_SLOT), lambda b, g: (b, 1)),
                  pl.BlockSpec((ctx_len, D_NA), lambda b, g: (b, 1)),
                  pl.BlockSpec((ctx_len, NA_HEADS * V_SLOT), lambda b, g: (b, 1)),
                  pl.BlockSpec((1, NA_HEADS, nq, nk), lambda b, g: (sel(g), 0, 0, 0))],
        out_specs=pl.BlockSpec((nq, D_NA), lambda b, g: (b * n_groups + g, 0)),
        out_shape=jax.ShapeDtypeStruct((batch * seq, D_NA), BF16),
        compiler_params=_params(("parallel", "arbitrary")),
        name="neighbourhood_attention",
    )(qkv, qkv, qkv, qkv_c, qkv_c, bias)


def _ctx_attn_kernel(q_ref, k_ref, v_ref, o_ref):
    for h in range(NA_HEADS):
        sl = slice(h * NA_HEAD_DIM, (h + 1) * NA_HEAD_DIM)
        o = _softmax_pv([_nt_dot(q_ref[:, sl], k_ref[:, sl])], [v_ref[:, h * V_SLOT:(h + 1) * V_SLOT]])
        o_ref[:, sl] = o.astype(o_ref.dtype)


def _context_attention(qkv_c, batch, ctx_len):
    spec = lambda j: pl.BlockSpec((ctx_len, D_NA), lambda b: (b, j))
    return pl.pallas_call(
        _ctx_attn_kernel,
        grid=(batch,),
        in_specs=[spec(0), spec(1), pl.BlockSpec((ctx_len, NA_HEADS * V_SLOT), lambda b: (b, 1))],
        out_specs=spec(0),
        out_shape=jax.ShapeDtypeStruct((batch * ctx_len, D_NA), BF16),
        compiler_params=_params(("parallel",)),
        name="context_attention",
    )(qkv_c, qkv_c, qkv_c)


SUBLANES = 8
LANES = 128
SHORT_CHUNK = 128
DW_CHUNK = 128
DW_STAGES = 2
HY_HALO = 8
CF_HALO = 16


def _conv_kernel(hy_ref, cf_ref, sw_ref, sb_ref, dw_ref, db_ref, lg_ref, lb_ref,
                 x1_ref, x2_ref, v_ref, cfo_ref, hpad, gpad, ph_scr, *, seq):
    hpad[0:HY_HALO, :] = jnp.zeros((HY_HALO, D_HYIN), F32)
    hpad[seq + HY_HALO:seq + 2 * HY_HALO, :] = jnp.zeros((HY_HALO, D_HYIN), F32)
    hpad[HY_HALO:seq + HY_HALO, :] = hy_ref[...]
    gpad[0:CF_HALO, :] = jnp.zeros((CF_HALO, D_CF), F32)
    gpad[seq + CF_HALO:seq + 2 * CF_HALO, :] = jnp.zeros((CF_HALO, D_CF), F32)
    gpad[CF_HALO:seq + CF_HALO, :] = cf_ref[:, :D_CF] * jax.nn.sigmoid(cf_ref[:, D_CF:])
    pad_s = 1
    pad_c = (CF_KERNEL - 1) // 2

    def short_body(i, carry):
        ch = SHORT_CHUNK
        base = pl.multiple_of(i * ch, ch)
        win = hpad[pl.ds(base, ch + 2 * HY_HALO), :]
        u = sb_ref[...]
        for t in range(3):
            o = HY_HALO - pad_s + t
            u = u + win[o:o + ch, :] * sw_ref[t:t + 1, :]
        x1_ref[i] = u[:, :D_HY].T
        x2_ref[i] = u[:, D_HY:2 * D_HY].T
        v_ref[i] = u[:, 2 * D_HY:].T
        return carry

    lax.fori_loop(0, seq // SHORT_CHUNK, short_body, 0)

    def dw_chunk(base, stage):
        ch = DW_CHUNK
        gw = gpad[pl.ds(base, ch + 2 * CF_HALO), :]
        span = ch + 2 * CF_HALO - SUBLANES
        for phase in range(1, SUBLANES):
            ph_scr[stage, phase - 1] = gw[phase:phase + span, :]
        y = db_ref[...]
        for t in range(CF_KERNEL):
            o = CF_HALO - pad_c + t
            phase = o % SUBLANES
            if phase == 0:
                src = gpad[pl.ds(base + o, ch), :]
            else:
                src = ph_scr[stage, phase - 1, o - phase:o - phase + ch, :]
            y = y + src * dw_ref[t:t + 1, :]
        mu = jnp.mean(y, axis=-1, keepdims=True)
        yc = y - mu
        var = jnp.mean(yc * yc, axis=-1, keepdims=True)
        yn = yc * lax.rsqrt(var + EPS) * lg_ref[...] + lb_ref[...]
        cfo_ref[pl.ds(base, ch), :] = (yn * jax.nn.sigmoid(yn)).astype(cfo_ref.dtype)

    def dw_body(i, carry):
        for stage in range(DW_STAGES):
            dw_chunk(pl.multiple_of((i * DW_STAGES + stage) * DW_CHUNK, DW_CHUNK), stage)
        return carry

    lax.fori_loop(0, seq // (DW_CHUNK * DW_STAGES), dw_body, 0)


def _conv_mixers(hy_in, cf_in, sw, sb, dw, db, lg, lb, batch, seq):
    full = lambda a: pl.BlockSpec(a.shape, lambda b: (0, 0))
    cm = pl.BlockSpec((seq // LANES, None, D_HY, LANES), lambda b: (0, b, 0, 0))
    cm_shape = jax.ShapeDtypeStruct((seq // LANES, batch, D_HY, LANES), F32)
    return pl.pallas_call(
        functools.partial(_conv_kernel, seq=seq),
        grid=(batch,),
        in_specs=[pl.BlockSpec((seq, D_HYIN), lambda b: (b, 0)), pl.BlockSpec((seq, D_CFIN), lambda b: (b, 0)),
                  full(sw), full(sb), full(dw), full(db), full(lg), full(lb)],
        out_specs=[cm, cm, cm, pl.BlockSpec((seq, D_CF), lambda b: (b, 0))],
        out_shape=[cm_shape, cm_shape, cm_shape, jax.ShapeDtypeStruct((batch * seq, D_CF), BF16)],
        scratch_shapes=[pltpu.VMEM((seq + 2 * HY_HALO, D_HYIN), F32), pltpu.VMEM((seq + 2 * CF_HALO, D_CF), F32),
                        pltpu.VMEM((DW_STAGES, SUBLANES - 1, DW_CHUNK + 2 * CF_HALO - SUBLANES, D_CF), F32)],
        compiler_params=_params(("parallel",)),
        name="conv_mixers",
    )(hy_in, cf_in, sw, sb, dw, db, lg, lb)


def _filter_kernel(fb_ref, w1t_ref, w1c_ref, w1s_ref, b1_ref, fr_ref, w2_ref, b2_ref, w3_ref, rate_ref, o_ref, *, seq):
    pos = lax.broadcasted_iota(I32, (seq, 1), 0).astype(F32)
    t = pos / max(seq - 1, 1)
    ang = fb_ref[...] * (2.0 * math.pi) * pos / seq
    dot = functools.partial(jnp.dot, preferred_element_type=F32, precision=HIGHEST)
    fr = fr_ref[...]
    z1 = t * w1t_ref[...] + dot(jnp.cos(ang), w1c_ref[...]) + dot(-jnp.sin(ang), w1s_ref[...]) + b1_ref[...]
    h = jnp.sin(fr * z1)
    h = jnp.sin(fr * (dot(h, w2_ref[...]) + b2_ref[...]))
    o_ref[...] = dot(h, w3_ref[...]) * jnp.exp(-t * rate_ref[...])


def _hyena_filters(seq, w1, b1, freq, w2, b2, w3):
    bands = (HY_EMB - 1) // 2
    fb = jnp.linspace(1e-4, bands - 1, bands, dtype=F32)[None, :]
    min_decay = math.log(HY_TARGET) / HY_SLOW_DECAY
    max_decay = math.log(HY_TARGET) / HY_FAST_DECAY
    rate = jnp.abs(jnp.linspace(min_decay, max_decay, D_HY, dtype=F32))
    rate4 = jnp.tile(rate, 4)[None, :]
    args = (fb, w1[0:1], w1[1:1 + bands], w1[1 + bands:], b1[None, :], freq[None, :], w2, b2[None, :], w3, rate4)
    h = pl.pallas_call(
        functools.partial(_filter_kernel, seq=seq),
        out_shape=jax.ShapeDtypeStruct((seq, 4 * D_HY), F32),
        compiler_params=_params(None),
        name="hyena_filter",
    )(*args)
    h = h.reshape(seq, 2, 2, D_HY)
    fwd, bwd = h[:, 0], h[:, 1]
    line = jnp.concatenate([jnp.zeros_like(fwd[:1]), bwd[1:][::-1], fwd], axis=0)
    return jnp.transpose(line, (1, 2, 0))


HY_CG = 8


def _hyena_kernel(d_ref, x1_ref, x2_ref, v_ref, kf_ref, o_ref, w_scr, acc_scr, *, batch, seq):
    nb = seq // HY_T
    tiles = HY_T // LANES
    cg = pl.program_id(0)

    def rows_of(ref):
        return ref.reshape(seq // LANES * batch * HY_CG, LANES)

    def tile_rows(k, c):
        return pl.ds((k * batch) * HY_CG + c, batch, stride=HY_CG)

    def blocks(ref, c):
        rows = rows_of(ref)
        return jnp.concatenate(
            [jnp.concatenate([rows[tile_rows(j * tiles + h, c), :] for h in range(tiles)], axis=1)
             for j in range(nb)], axis=0)

    out_rows = rows_of(o_ref)
    for c in range(HY_CG):
        z = blocks(v_ref, c)
        for n, gate_ref in enumerate((x1_ref, x2_ref)):
            line = jnp.broadcast_to(kf_ref[n, c:c + 1, :], (HY_T, 2 * seq))
            w_scr[...] = pltpu.roll(line, 0, 1, stride=1, stride_axis=0).astype(BF16)
            zb = z.astype(BF16)
            acc_scr[...] = jnp.dot(zb, w_scr[:, seq:seq + HY_T], preferred_element_type=F32)
            for d in range(1, nb):
                m = batch * (nb - d)
                acc_scr[batch * d:, :] += jnp.dot(zb[:m], w_scr[:, seq + d * HY_T:seq + (d + 1) * HY_T],
                                                  preferred_element_type=F32)
                acc_scr[:m, :] += jnp.dot(zb[batch * d:], w_scr[:, seq - d * HY_T:seq - (d - 1) * HY_T],
                                          preferred_element_type=F32)
            z = blocks(gate_ref, c) * (acc_scr[...] + d_ref[n, cg * HY_CG + c] * z)
        for j in range(nb):
            for h in range(tiles):
                out_rows[tile_rows(j * tiles + h, c), :] = z[j * batch:(j + 1) * batch, h * LANES:(h + 1) * LANES]


def _hyena_long_conv(x1t, x2t, vt, kf, bias_d, batch, seq):
    cm = pl.BlockSpec((seq // LANES, batch, HY_CG, LANES), lambda g: (0, 0, g, 0))
    return pl.pallas_call(
        functools.partial(_hyena_kernel, batch=batch, seq=seq),
        grid=(D_HY // HY_CG,),
        in_specs=[pl.BlockSpec(memory_space=pltpu.SMEM), cm, cm, cm,
                  pl.BlockSpec((2, HY_CG, 2 * seq), lambda g: (0, g, 0))],
        out_specs=cm,
        out_shape=jax.ShapeDtypeStruct((seq // LANES, batch, D_HY, LANES), F32),
        scratch_shapes=[pltpu.VMEM((HY_T, 2 * seq), BF16), pltpu.VMEM((seq // HY_T * batch, HY_T), F32)],
        compiler_params=_params(("parallel",)),
        name="hyena_long_conv",
    )(bias_d, x1t, x2t, vt, kf)


def _outproj_kernel(attn_ref, hy_ref, cf_ref, x_ref, g1_ref, w_ref, ng_ref, sc_ref, sh_ref, rw_ref,
                    xo_ref, h2_ref, lt_ref):
    y = jnp.dot(attn_ref[...], w_ref[:D_NA], preferred_element_type=F32)
    hy = jnp.concatenate([hy_ref[k].T for k in range(hy_ref.shape[0])], axis=0).astype(BF16)
    y = y + jnp.dot(hy, w_ref[D_NA:D_NA + D_HY], preferred_element_type=F32)
    y = y + jnp.dot(cf_ref[...], w_ref[D_NA + D_HY:], preferred_element_type=F32)
    xn = x_ref[...] + g1_ref[0] * y
    xo_ref[...] = xn
    h2 = _rms_mod(xn, ng_ref[...], sc_ref[0], sh_ref[0]).astype(BF16)
    h2_ref[...] = h2
    lt_ref[0] = _nt_dot(rw_ref[...], h2)


def _out_projection(attn, hyt, cf, x2d, g1, w_bf, ng, sc, sh, rw_t, batch, seq, tm):
    n = x2d.shape[0]
    tps = seq // tm
    row = lambda w: pl.BlockSpec((tm, w), lambda i: (i, 0))
    full = lambda a: pl.BlockSpec(a.shape, lambda i: (0, 0))
    return pl.pallas_call(
        _outproj_kernel,
        grid=(n // tm,),
        in_specs=[row(D_NA), pl.BlockSpec((tm // LANES, None, D_HY, LANES), lambda i: (i % tps, i // tps, 0, 0)),
                  row(D_CF), row(D_MODEL),
                  _per_sample_spec(g1, tps), full(w_bf), full(ng), _per_sample_spec(sc, tps),
                  _per_sample_spec(sh, tps), full(rw_t)],
        out_specs=[row(D_MODEL), row(D_MODEL), pl.BlockSpec((1, N_EXPERTS, tm), lambda i: (i // tps, 0, i % tps))],
        out_shape=[jax.ShapeDtypeStruct((n, D_MODEL), F32), jax.ShapeDtypeStruct((n, D_MODEL), BF16),
                   jax.ShapeDtypeStruct((batch, N_EXPERTS, seq), F32)],
        compiler_params=_params(("parallel",)),
        name="out_projection",
    )(attn, hyt, cf, x2d, g1, w_bf, ng, sc, sh, rw_t)


def _route_kernel(lt_ref, tri_ref, tcol_ref, pos_ref, aff_ref, cnt_ref, *, cap):
    lg = lt_ref[...]
    bs, _, seq = lg.shape
    e = jnp.exp(lg - jnp.max(lg, axis=1, keepdims=True))
    aff = e / jnp.sum(e, axis=1, keepdims=True)
    aff_ref[...] = aff
    keys = pltpu.bitcast(aff.reshape(bs * N_EXPERTS, seq), I32)

    def body(i, thr):
        cand = thr | (jnp.int32(1) << (30 - i))
        cnt = jnp.sum(jnp.where(keys >= cand, 1.0, 0.0), axis=1, keepdims=True)
        return jnp.where(cnt >= cap, cand, thr)

    thr = lax.fori_loop(0, 31, body, jnp.zeros((bs * N_EXPERTS, 1), I32))
    gt = keys > thr
    eq = keys == thr
    need = cap - jnp.sum(jnp.where(gt, 1.0, 0.0), axis=1, keepdims=True)
    rank = jnp.dot(jnp.where(eq, 1.0, 0.0).astype(BF16), tri_ref[...], preferred_element_type=F32)
    sel = gt | (eq & (rank < need))
    sel_bf = jnp.where(sel, 1.0, 0.0).astype(BF16)
    slot = jnp.dot(sel_bf, tri_ref[...], preferred_element_type=F32)
    pos_ref[...] = jnp.where(sel, slot, -1.0).astype(I32).reshape(bs, N_EXPERTS, seq)
    cnt = jnp.dot(sel_bf, tcol_ref[...], preferred_element_type=F32)
    cnt_ref[...] = cnt.astype(I32).reshape(bs, N_EXPERTS, LANES)


def _routing(logits_t, cap, tl):
    batch, _, seq = logits_t.shape
    bs = 8 if batch % 8 == 0 else batch
    idx = np.arange(seq)
    tri = jnp.asarray(idx[:, None] < idx[None, :], dtype=BF16)
    tile_start = np.minimum(np.arange(LANES) * tl, seq)
    tcol = jnp.asarray(idx[:, None] < tile_start[None, :], dtype=BF16)
    blk = pl.BlockSpec((bs, N_EXPERTS, seq), lambda i: (i, 0, 0))
    cblk = pl.BlockSpec((bs, N_EXPERTS, LANES), lambda i: (i, 0, 0))
    pos, aff, cnt = pl.pallas_call(
        functools.partial(_route_kernel, cap=cap),
        grid=(batch // bs,),
        in_specs=[blk, pl.BlockSpec((seq, seq), lambda i: (0, 0)), pl.BlockSpec((seq, LANES), lambda i: (0, 0))],
        out_specs=[blk, blk, cblk],
        out_shape=[jax.ShapeDtypeStruct((batch, N_EXPERTS, seq), I32),
                   jax.ShapeDtypeStruct((batch, N_EXPERTS, seq), F32),
                   jax.ShapeDtypeStruct((batch, N_EXPERTS, LANES), I32)],
        compiler_params=_params(("parallel",)),
        name="ec_routing",
    )(logits_t, tri, tcol)
    return pos, aff, cnt[:, :, :seq // tl]


ROUTE_TILE = 512
SLOT_ALIGN = 16
GATHER_GROUP = 4


def _slot_window(cnt_ref, sample, expert, tile, n_tiles, cap, window):
    base = (sample * N_EXPERTS + expert) * n_tiles + tile
    lo = cnt_ref[base]
    hi = cnt_ref[base + 1] if tile + 1 < n_tiles else cap
    start = jnp.minimum(lo // SLOT_ALIGN * SLOT_ALIGN, cap - window)
    return pl.multiple_of(start, SLOT_ALIGN), hi - start <= window


def _gather_kernel(cnt_ref, pos_ref, h_ref, xe_ref, oh_scr, *, cap, window, tl):
    b = pl.program_id(0)
    seq = h_ref.shape[0]
    n_t = seq // tl

    def gather_all_tokens():
        slot = lax.broadcasted_iota(I32, (cap, seq), 0)
        for e in range(N_EXPERTS):
            onehot = jnp.where(pos_ref[0, e:e + 1, :] == slot, 1.0, 0.0).astype(BF16)
            xe_ref[0, e * cap:(e + 1) * cap, :] = jnp.dot(onehot, h_ref[...],
                                                          preferred_element_type=F32).astype(BF16)

    if window == cap:
        gather_all_tokens()
        return

    starts, fits = {}, True
    for e in range(N_EXPERTS):
        for t in range(n_t):
            starts[e, t], ok = _slot_window(cnt_ref, b, e, t, n_t, cap, window)
            fits = jnp.logical_and(fits, ok)

    @pl.when(fits)
    def _():
        xe_ref[...] = jnp.zeros(xe_ref.shape, BF16)
        slot = lax.broadcasted_iota(I32, (window, tl), 0)
        for t in range(n_t):
            for g in range(N_EXPERTS // GATHER_GROUP):
                for i in range(GATHER_GROUP):
                    e = g * GATHER_GROUP + i
                    hit = pos_ref[0, e:e + 1, t * tl:(t + 1) * tl] == slot + starts[e, t]
                    oh_scr[i * window:(i + 1) * window, :] = jnp.where(hit, 1.0, 0.0).astype(BF16)
                rows = jnp.dot(oh_scr[...], h_ref[t * tl:(t + 1) * tl, :], preferred_element_type=F32).astype(BF16)
                for i in range(GATHER_GROUP):
                    e = g * GATHER_GROUP + i
                    row0 = pl.multiple_of(e * cap + starts[e, t], SLOT_ALIGN)
                    xe_ref[0, pl.ds(row0, window), :] += rows[i * window:(i + 1) * window]

    @pl.when(jnp.logical_not(fits))
    def _():
        gather_all_tokens()


def _gather_tokens(pos, cnt, h2, cap, tl):
    batch, _, seq = pos.shape
    window = cap // 2 if seq // tl > 1 else cap
    grid_spec = pltpu.PrefetchScalarGridSpec(
        num_scalar_prefetch=1,
        grid=(batch,),
        in_specs=[pl.BlockSpec((1, N_EXPERTS, seq), lambda b, c: (b, 0, 0)),
                  pl.BlockSpec((seq, D_MODEL), lambda b, c: (b, 0))],
        out_specs=pl.BlockSpec((1, N_EXPERTS * cap, D_MODEL), lambda b, c: (b, 0, 0)),
        scratch_shapes=[pltpu.VMEM((GATHER_GROUP * window, tl), BF16)])
    return pl.pallas_call(
        functools.partial(_gather_kernel, cap=cap, window=window, tl=tl),
        grid_spec=grid_spec,
        out_shape=jax.ShapeDtypeStruct((batch, N_EXPERTS * cap, D_MODEL), BF16),
        compiler_params=_params(("parallel",)),
        name="ec_gather",
    )(cnt.reshape(-1), pos, h2)


FFN_ROWS = 512


def _ffn_kernel(*refs, layer, n_ctx_tiles):
    if n_ctx_tiles:
        xe_ref, xc_ref, w1_hbm, w3_hbm, w2_hbm, y_ref, yc_ref, w1b, w3b, w2b, st1, st3, st2, sem = refs
    else:
        xe_ref, w1_hbm, w3_hbm, w2_hbm, y_ref, w1b, w3b, w2b, st1, st3, st2, sem = refs
    e, rt = pl.program_id(0), pl.program_id(1)
    n_e, n_rt = N_EXPERTS, pl.num_programs(1)
    r1, r2 = st1.shape[1], st2.shape[1]

    def slab_copies(expert, k, slot):
        o1, o2 = pl.multiple_of(k * r1, r1), pl.multiple_of(k * r2, r2)
        return (pltpu.make_async_copy(w1_hbm.at[layer, expert, pl.ds(o1, r1), :], st1.at[slot], sem.at[slot, 0]),
                pltpu.make_async_copy(w3_hbm.at[layer, expert, pl.ds(o1, r1), :], st3.at[slot], sem.at[slot, 1]),
                pltpu.make_async_copy(w2_hbm.at[layer, expert, pl.ds(o2, r2), :], st2.at[slot], sem.at[slot, 2]))

    def start_slab(expert, k, slot):
        for cp in slab_copies(expert, k, slot):
            cp.start()

    def cast_slab(k, slot, wslot):
        o1, o2 = pl.multiple_of(k * r1, r1), pl.multiple_of(k * r2, r2)
        w1b[wslot, pl.ds(o1, r1), :] = st1[slot].astype(BF16)
        w3b[wslot, pl.ds(o1, r1), :] = st3[slot].astype(BF16)
        w2b[wslot, pl.ds(o2, r2), :] = st2[slot].astype(BF16)

    def land_slab(expert, k, slot, wslot):
        for cp in slab_copies(expert, k, slot):
            cp.wait()
        cast_slab(k, slot, wslot)

    def prefetched_expert(expert):
        return jnp.minimum(expert + 1, n_e - 1)

    n_slabs = D_MODEL // r1

    @pl.when((e == 0) & (rt == 0))
    def _():
        start_slab(0, 0, 0)
        for k in range(n_slabs):
            if k + 1 < n_slabs:
                start_slab(0, k + 1, (k + 1) % 2)
            land_slab(0, k, k % 2, 0)
        start_slab(prefetched_expert(0), 0, 0)

    slot = (e * n_rt + rt) % 2
    for cp in slab_copies(prefetched_expert(e), rt, slot):
        cp.wait()
    last_tile = rt + 1 == n_rt
    nxt_e = jnp.minimum(jnp.where(last_tile, e + 1, e), n_e - 1)
    nxt_k = jnp.where(last_tile, 0, rt + 1)
    start_slab(prefetched_expert(nxt_e), nxt_k, 1 - slot)
    cast_slab(rt, slot, (e + 1) % 2)
    ws = e % 2

    def swiglu(x_ref, o_ref):
        bt, cap, _ = x_ref.shape
        x = x_ref[...].reshape(bt * cap, D_MODEL)
        a = jnp.dot(x, w1b[ws], preferred_element_type=F32)
        u = jnp.dot(x, w3b[ws], preferred_element_type=F32)
        h = (a * jax.nn.sigmoid(a) * u).astype(BF16)
        o_ref[...] = jnp.dot(h, w2b[ws], preferred_element_type=F32).astype(BF16).reshape(bt, cap, D_MODEL)

    swiglu(xe_ref, y_ref)
    if n_ctx_tiles:
        @pl.when(rt < n_ctx_tiles)
        def _():
            swiglu(xc_ref, yc_ref)

    @pl.when((e == n_e - 1) & last_tile)
    def _():
        for cp in slab_copies(prefetched_expert(nxt_e), nxt_k, 1 - slot):
            cp.wait()


def _tile_samples(batch, cap):
    bt = max(1, min(batch, FFN_ROWS // cap))
    while batch % bt:
        bt -= 1
    return bt


def _expert_ffn(xe, xe_ctx, w1, w3, w2, layer):
    batch = xe.shape[0]
    cap = xe.shape[1] // N_EXPERTS
    bt = _tile_samples(batch, cap)
    n_rt = batch // bt
    xspec = pl.BlockSpec((bt, cap, D_MODEL), lambda e, r: (r, e, 0))
    hbm = pl.BlockSpec(memory_space=pl.ANY)
    in_specs, out_specs, out_shape, args, n_ctx_tiles = [xspec], [xspec], [jax.ShapeDtypeStruct(xe.shape, BF16)], [xe], 0
    if xe_ctx is not None:
        cap_c = xe_ctx.shape[1] // N_EXPERTS
        bt_c = _tile_samples(batch, cap_c)
        n_ctx_tiles = batch // bt_c
        assert n_ctx_tiles <= n_rt
        cspec = pl.BlockSpec((bt_c, cap_c, D_MODEL), lambda e, r: (jnp.minimum(r, n_ctx_tiles - 1), e, 0))
        in_specs.append(cspec)
        out_specs.append(cspec)
        out_shape.append(jax.ShapeDtypeStruct(xe_ctx.shape, BF16))
        args.append(xe_ctx)
    assert D_MODEL % n_rt == 0 and EXPERT_FF % n_rt == 0
    r1, r2 = D_MODEL // n_rt, EXPERT_FF // n_rt
    out = pl.pallas_call(
        functools.partial(_ffn_kernel, layer=layer, n_ctx_tiles=n_ctx_tiles),
        grid=(N_EXPERTS, n_rt),
        in_specs=in_specs + [hbm, hbm, hbm],
        out_specs=out_specs,
        out_shape=out_shape,
        scratch_shapes=[pltpu.VMEM((2, D_MODEL, EXPERT_FF), BF16), pltpu.VMEM((2, D_MODEL, EXPERT_FF), BF16),
                        pltpu.VMEM((2, EXPERT_FF, D_MODEL), BF16), pltpu.VMEM((2, r1, EXPERT_FF), F32),
                        pltpu.VMEM((2, r1, EXPERT_FF), F32), pltpu.VMEM((2, r2, D_MODEL), F32),
                        pltpu.SemaphoreType.DMA((2, 3))],
        compiler_params=_params(("arbitrary", "arbitrary")),
        name="ec_expert_ffn",
    )(*args, w1, w3, w2)
    return (out[0], out[1]) if xe_ctx is not None else (out[0], None)


def _scatter_kernel(cnt_ref, pos_ref, aff_ref, y_ref, x_ref, g2_ref, fg_ref, o_ref, pg_scr, yw_scr, *,
                    cap, window, final_norm):
    b, t = pl.program_id(0), pl.program_id(1)
    n_t = pl.num_programs(1)
    tl = x_ref.shape[0]

    def finish(moe):
        xn = x_ref[...] + g2_ref[0] * moe
        if final_norm:
            xn = xn * lax.rsqrt(jnp.mean(xn * xn, axis=-1, keepdims=True) + EPS) * fg_ref[...]
        o_ref[...] = xn

    def scatter_all_slots():
        slot = lax.broadcasted_iota(I32, (cap, tl), 0)
        for e in range(N_EXPERTS):
            hit = pos_ref[0, e:e + 1, :] == slot
            pg_scr[e * cap:(e + 1) * cap, :] = jnp.where(hit, aff_ref[0, e:e + 1, :], 0.0).astype(BF16)
        finish(_tn_dot(pg_scr[...], y_ref[0]))

    if window == cap:
        scatter_all_slots()
        return

    starts, fits = [], True
    for e in range(N_EXPERTS):
        base = (b * N_EXPERTS + e) * n_t + t
        lo = cnt_ref[base]
        hi = jnp.where(t + 1 < n_t, cnt_ref[jnp.minimum(base + 1, cnt_ref.shape[0] - 1)], cap)
        start = jnp.minimum(lo // SLOT_ALIGN * SLOT_ALIGN, cap - window)
        starts.append(pl.multiple_of(start, SLOT_ALIGN))
        fits = jnp.logical_and(fits, hi - start <= window)

    @pl.when(fits)
    def _():
        slot = lax.broadcasted_iota(I32, (window, tl), 0)
        for e in range(N_EXPERTS):
            hit = pos_ref[0, e:e + 1, :] == slot + starts[e]
            pg_scr[e * window:(e + 1) * window, :] = jnp.where(hit, aff_ref[0, e:e + 1, :], 0.0).astype(BF16)
            row0 = pl.multiple_of(e * cap + starts[e], SLOT_ALIGN)
            yw_scr[e * window:(e + 1) * window, :] = y_ref[0, pl.ds(row0, window), :]
        finish(_tn_dot(pg_scr[:N_EXPERTS * window, :], yw_scr[...]))

    @pl.when(jnp.logical_not(fits))
    def _():
        scatter_all_slots()


def _scatter_residual(pos, aff, cnt, y, x2d, g2, fg, cap, tl, final_norm):
    batch, _, seq = pos.shape
    tps = seq // tl
    window = cap // 2 if tps > 1 else cap
    rspec = pl.BlockSpec((1, N_EXPERTS, tl), lambda b, t, c: (b, 0, t))
    xspec = pl.BlockSpec((tl, D_MODEL), lambda b, t, c: (b * tps + t, 0))
    g2spec = (pl.BlockSpec((1, 1, D_MODEL), lambda b, t, c: (0, 0, 0)) if g2.shape[0] == 1
              else pl.BlockSpec((1, 1, D_MODEL), lambda b, t, c: (b, 0, 0)))
    grid_spec = pltpu.PrefetchScalarGridSpec(
        num_scalar_prefetch=1,
        grid=(batch, tps),
        in_specs=[rspec, rspec, pl.BlockSpec((1, N_EXPERTS * cap, D_MODEL), lambda b, t, c: (b, 0, 0)), xspec,
                  g2spec, pl.BlockSpec((1, D_MODEL), lambda b, t, c: (0, 0))],
        out_specs=xspec,
        scratch_shapes=[pltpu.VMEM((N_EXPERTS * cap, tl), BF16), pltpu.VMEM((N_EXPERTS * window, D_MODEL), BF16)])
    return pl.pallas_call(
        functools.partial(_scatter_kernel, cap=cap, window=window, final_norm=final_norm),
        grid_spec=grid_spec,
        out_shape=jax.ShapeDtypeStruct(x2d.shape, F32),
        compiler_params=_params(("parallel", "arbitrary")),
        name="ec_scatter_residual",
    )(cnt.reshape(-1), pos, aff, y, x2d, g2, fg)


def _stream_front(x2d, mods, lw, batch, seq, tm, attn_fn):
    sh1, sc1, g1, sh2, sc2, g2 = mods
    qkv, hy_in, cf_in = _in_projection(x2d, lw["norm1_g"], sc1, sh1, lw["w_in"], seq, tm)
    attn = attn_fn(qkv)
    x1t, x2t, vt, cf = _conv_mixers(hy_in, cf_in, lw["hy_short_w"], lw["hy_short_b"], lw["cf_dw_w"], lw["cf_dw_b"],
                                    lw["cf_ln_g"], lw["cf_ln_b"], batch, seq)
    kf = _hyena_filters(seq, lw["hy_filt_w1"], lw["hy_filt_b1"], lw["hy_filt_freq"], lw["hy_filt_w2"],
                        lw["hy_filt_b2"], lw["hy_filt_w3"])
    hyt = _hyena_long_conv(x1t, x2t, vt, kf, lw["hy_bias_d"], batch, seq)
    x_mid, h2, logits_t = _out_projection(attn, hyt, cf, x2d, g1, lw["w_out"], lw["norm2_g"], sc2, sh2,
                                          lw["router_wt"], batch, seq, tm)
    cap = EC_CAPACITY * seq // N_EXPERTS
    pos, aff, cnt = _routing(logits_t, cap, min(seq, ROUTE_TILE))
    xe = _gather_tokens(pos, cnt, h2, cap, min(seq, ROUTE_TILE))
    return dict(pos=pos, aff=aff, cnt=cnt, xe=xe, x_mid=x_mid, g2=g2, cap=cap, seq=seq, qkv=qkv)


def _stream_back(front, y, lw, final_norm):
    return _scatter_residual(front["pos"], front["aff"], front["cnt"], y, front["x_mid"], front["g2"],
                             lw["final_norm_g"], front["cap"], min(front["seq"], ROUTE_TILE), final_norm)


def kernel(x, c, ctx, c_ctx, w_mod, b_mod, norm1_g, norm2_g, w_in, na_rpb, hy_short_w, hy_short_b, hy_filt_w1,
           hy_filt_b1, hy_filt_freq, hy_filt_w2, hy_filt_b2, hy_filt_w3, hy_bias_d, cf_dw_w, cf_dw_b, cf_ln_g,
           cf_ln_b, w_out, router_w, expert_w1, expert_w3, expert_w2, final_norm_g):
    batch, seq, _ = x.shape
    ctx_len = ctx.shape[1]
    depth = w_mod.shape[0]
    rows = seq // GRID_W

    n_c = batch + 1
    n_c_pad = -(-n_c // 8) * 8
    cvec = jnp.concatenate([c, c_ctx[None, :], jnp.zeros((n_c_pad - n_c, D_MODEL), F32)], axis=0)
    mod_all = _modulation(cvec, w_mod, b_mod)

    xl = x.reshape(batch * seq, D_MODEL)
    xc = ctx.reshape(batch * ctx_len, D_MODEL)
    for l in range(depth):
        last = l == depth - 1
        lw = dict(
            norm1_g=norm1_g[l][None, :], norm2_g=norm2_g[l][None, :], w_in=w_in[l].astype(BF16),
            hy_short_w=hy_short_w[l], hy_short_b=hy_short_b[l][None, :], hy_filt_w1=hy_filt_w1[l],
            hy_filt_b1=hy_filt_b1[l], hy_filt_freq=hy_filt_freq[l], hy_filt_w2=hy_filt_w2[l],
            hy_filt_b2=hy_filt_b2[l], hy_filt_w3=hy_filt_w3[l], hy_bias_d=hy_bias_d[l], cf_dw_w=cf_dw_w[l],
            cf_dw_b=cf_dw_b[l][None, :], cf_ln_g=cf_ln_g[l][None, :], cf_ln_b=cf_ln_b[l][None, :],
            w_out=w_out[l].astype(BF16), router_wt=router_w[l].T.astype(BF16),
            expert_w1=expert_w1, expert_w3=expert_w3, expert_w2=expert_w2, layer=l,
            final_norm_g=final_norm_g[None, :])
        chunks = [mod_all[l, :, j * D_MODEL:(j + 1) * D_MODEL] for j in range(N_MOD)]
        mods_l = [m[:batch, None, :] for m in chunks]
        mods_c = [m[batch:batch + 1, None, :] for m in chunks]
        bias = _na_bias_tiles(na_rpb[l], rows)

        if last:
            front_c = None
            qkv_c, _, _ = _in_projection(xc, lw["norm1_g"], mods_c[1], mods_c[0], lw["w_in"], ctx_len, ctx_len)
        else:
            front_c = _stream_front(xc, mods_c, lw, batch, ctx_len, ctx_len,
                                    lambda qkv: _context_attention(qkv, batch, ctx_len))
            qkv_c = front_c["qkv"]
        front_l = _stream_front(xl, mods_l, lw, batch, seq, 512,
                                lambda qkv: _neighbourhood_attention(qkv, qkv_c, bias, batch, seq, ctx_len))
        y_l, y_c = _expert_ffn(front_l["xe"], None if last else front_c["xe"], expert_w1, expert_w3, expert_w2, l)
        if not last:
            xc = _stream_back(front_c, y_c, lw, False)
        xl = _stream_back(front_l, y_l, lw, last)
    return xl.reshape(batch, seq, D_MODEL)
```

```python
import functools
import math

import numpy as np
import jax
import jax.numpy as jnp
from jax import lax
from jax.experimental import pallas as pl
from jax.experimental.pallas import tpu as pltpu

F32, BF16, I32 = jnp.float32, jnp.bfloat16, jnp.int32
HIGHEST = lax.Precision.HIGHEST

D_MODEL = 1024
GRID_W = 64
N_MOD = 6
EPS = 1e-6
NEG_INF = -1e30
NA_HEAD_DIM = 64
D_NA = 512
NA_HEADS = 8
NA_WIN_H = 8
NA_WIN_W = 16
D_HY = 256
HY_EMB = 33
HY_FAST_DECAY = 0.3
HY_SLOW_DECAY = 1.5
HY_TARGET = 1e-2
D_CF = 256
CF_KERNEL = 31
D_QKV = 3 * D_NA
V_SLOT = 128
D_QKVX = 2 * D_NA + NA_HEADS * V_SLOT
Q_SCALE = NA_HEAD_DIM ** -0.5 * math.log2(math.e)
D_HYIN = 3 * D_HY
D_CFIN = 2 * D_CF
N_EXPERTS = 16
EC_CAPACITY = 2
EXPERT_FF = 2 * D_MODEL

NA_QROWS = 4
NA_KROWS = NA_QROWS + NA_WIN_H - 1
HY_T = 256
MIB = 1024 * 1024
VMEM_LIMIT = 56 * MIB


def _params(sem, vmem=VMEM_LIMIT):
    return pltpu.CompilerParams(dimension_semantics=sem, vmem_limit_bytes=vmem)


def _nt_dot(a, b):
    return lax.dot_general(a, b, (((1,), (1,)), ((), ())), preferred_element_type=F32)


def _tn_dot(a, b):
    return lax.dot_general(a, b, (((0,), (0,)), ((), ())), preferred_element_type=F32)


def _mod_kernel(c_ref, w_ref, b_ref, o_ref):
    c = c_ref[...]
    s = c * jax.nn.sigmoid(c)
    o_ref[0] = jnp.dot(s, w_ref[0], preferred_element_type=F32, precision=HIGHEST) + b_ref[0]


def _modulation(cvec, w_mod, b_mod):
    depth, _, n = w_mod.shape
    rows = cvec.shape[0]
    tn = 1536
    return pl.pallas_call(
        _mod_kernel,
        grid=(depth, n // tn),
        in_specs=[pl.BlockSpec((rows, D_MODEL), lambda l, j: (0, 0)),
                  pl.BlockSpec((1, D_MODEL, tn), lambda l, j: (l, 0, j)),
                  pl.BlockSpec((1, 1, tn), lambda l, j: (l, 0, j))],
        out_specs=pl.BlockSpec((1, rows, tn), lambda l, j: (l, 0, j)),
        out_shape=jax.ShapeDtypeStruct((depth, rows, n), F32),
        compiler_params=_params(("arbitrary", "arbitrary")),
        name="adaln_mod",
    )(cvec, w_mod, b_mod.reshape(depth, 1, n))


def _rms_mod(x, g, sc, sh):
    y = x * lax.rsqrt(jnp.mean(x * x, axis=-1, keepdims=True) + EPS)
    return (y * g) * (1.0 + sc) + sh


def _inproj_kernel(x_ref, g_ref, sc_ref, sh_ref, w_ref, qkv_ref, hy_ref, cf_ref):
    h = _rms_mod(x_ref[...], g_ref[...], sc_ref[0], sh_ref[0]).astype(BF16)
    qkv_ref[:, :D_NA] = (jnp.dot(h, w_ref[:, :D_NA], preferred_element_type=F32) * Q_SCALE).astype(BF16)
    qkv_ref[:, D_NA:2 * D_NA] = jnp.dot(h, w_ref[:, D_NA:2 * D_NA], preferred_element_type=F32).astype(BF16)
    v = jnp.dot(h, w_ref[:, 2 * D_NA:D_QKV], preferred_element_type=F32).astype(BF16)
    lane = lax.broadcasted_iota(I32, (v.shape[0], V_SLOT - NA_HEAD_DIM), 1)
    ones_col = jnp.where(lane == 0, 1.0, 0.0).astype(BF16)
    for hd in range(NA_HEADS):
        lo = 2 * D_NA + hd * V_SLOT
        qkv_ref[:, lo:lo + NA_HEAD_DIM] = v[:, hd * NA_HEAD_DIM:(hd + 1) * NA_HEAD_DIM]
        qkv_ref[:, lo + NA_HEAD_DIM:lo + V_SLOT] = ones_col
    hy_ref[...] = jnp.dot(h, w_ref[:, D_QKV:D_QKV + D_HYIN], preferred_element_type=F32)
    cf_ref[...] = jnp.dot(h, w_ref[:, D_QKV + D_HYIN:], preferred_element_type=F32)


def _per_sample_spec(arr, tiles_per_sample):
    if arr.shape[0] == 1:
        return pl.BlockSpec((1, 1, D_MODEL), lambda i: (0, 0, 0))
    return pl.BlockSpec((1, 1, D_MODEL), lambda i: (i // tiles_per_sample, 0, 0))


def _in_projection(x2d, g, sc, sh, w_bf, seq, tm):
    n = x2d.shape[0]
    tps = seq // tm
    d_in = w_bf.shape[1]
    row = lambda w: pl.BlockSpec((tm, w), lambda i: (i, 0))
    return pl.pallas_call(
        _inproj_kernel,
        grid=(n // tm,),
        in_specs=[row(D_MODEL), pl.BlockSpec((1, D_MODEL), lambda i: (0, 0)),
                  _per_sample_spec(sc, tps), _per_sample_spec(sh, tps),
                  pl.BlockSpec((D_MODEL, d_in), lambda i: (0, 0))],
        out_specs=[row(D_QKVX), row(D_HYIN), row(D_CFIN)],
        out_shape=[jax.ShapeDtypeStruct((n, D_QKVX), BF16), jax.ShapeDtypeStruct((n, D_HYIN), F32),
                   jax.ShapeDtypeStruct((n, D_CFIN), F32)],
        compiler_params=_params(("parallel",)),
        name="in_projection",
    )(x2d, g, sc, sh, w_bf)


def _na_bias_tiles(rpb, rows):
    col = np.arange(GRID_W)
    c0 = np.clip(col - NA_WIN_W // 2, 0, GRID_W - NA_WIN_W)
    col_ok = (col[None, :] >= c0[:, None]) & (col[None, :] < c0[:, None] + NA_WIN_W)
    dc = np.clip(col[None, :] - col[:, None], 1 - NA_WIN_W, NA_WIN_W - 1) + NA_WIN_W - 1
    n_groups = rows // NA_QROWS
    n_dr, n_dc = 2 * NA_WIN_H - 1, 2 * NA_WIN_W - 1
    dr_hot = np.zeros((3, NA_QROWS, NA_KROWS, n_dr), np.float32)
    ok = np.zeros((3, NA_QROWS, GRID_W, NA_KROWS, GRID_W), bool)
    for v, grp in enumerate((0, 1, n_groups - 1)):
        r = grp * NA_QROWS + np.arange(NA_QROWS)
        k0 = np.clip(grp * NA_QROWS - NA_WIN_H // 2, 0, rows - NA_KROWS)
        kr = k0 + np.arange(NA_KROWS)
        r0 = np.clip(r - NA_WIN_H // 2, 0, rows - NA_WIN_H)
        row_ok = (kr[None, :] >= r0[:, None]) & (kr[None, :] < r0[:, None] + NA_WIN_H)
        dr = kr[None, :] - r[:, None] + NA_WIN_H - 1
        for i, j in zip(*np.nonzero(row_ok)):
            dr_hot[v, i, j, dr[i, j]] = 1.0
        ok[v] = row_ok[:, None, :, None] & col_ok[None, :, None, :]
    dc_hot = (dc[None, :, :] == np.arange(n_dc)[:, None, None]).astype(np.float32)
    t1 = jnp.einsum('hrd,dqk->hrqk', rpb.astype(F32), jnp.asarray(dc_hot), precision=HIGHEST)
    bias = jnp.einsum('vijr,hrqk->vhiqjk', jnp.asarray(dr_hot), t1, precision=HIGHEST)
    bias = jnp.where(jnp.asarray(ok)[:, None], bias * math.log2(math.e), NEG_INF)
    return bias.reshape(3, NA_HEADS, NA_QROWS * GRID_W, NA_KROWS * GRID_W)


def _softmax_pv(s_list, v_list):
    m = functools.reduce(jnp.maximum, [jnp.max(s, axis=-1, keepdims=True) for s in s_list])
    o = functools.reduce(jnp.add, [jnp.dot(jnp.exp2(s - m).astype(BF16), v, preferred_element_type=F32)
                                   for s, v in zip(s_list, v_list)])
    return o[:, :NA_HEAD_DIM] / o[:, NA_HEAD_DIM:NA_HEAD_DIM + 1]


def _na_kernel(q_ref, k_ref, v_ref, kc_ref, vc_ref, bias_ref, o_ref, *, rows):
    g = pl.program_id(1)
    k0 = jnp.clip(g * NA_QROWS - NA_WIN_H // 2, 0, rows - NA_KROWS) * GRID_W
    k0 = pl.multiple_of(k0, GRID_W)
    nk = NA_KROWS * GRID_W
    def scores(h):
        sl = slice(h * NA_HEAD_DIM, (h + 1) * NA_HEAD_DIM)
        qh = q_ref[:, sl]
        return [_nt_dot(qh, k_ref[pl.ds(k0, nk), sl]) + bias_ref[0, h], _nt_dot(qh, kc_ref[:, sl])]

    s = scores(0)
    for h in range(NA_HEADS):
        s_next = scores(h + 1) if h + 1 < NA_HEADS else None
        vs = slice(h * V_SLOT, (h + 1) * V_SLOT)
        o = _softmax_pv(s, [v_ref[pl.ds(k0, nk), vs], vc_ref[:, vs]])
        o_ref[:, h * NA_HEAD_DIM:(h + 1) * NA_HEAD_DIM] = o.astype(o_ref.dtype)
        s = s_next


def _neighbourhood_attention(qkv, qkv_c, bias, batch, seq, ctx_len):
    rows = seq // GRID_W
    n_groups = rows // NA_QROWS
    nq, nk = NA_QROWS * GRID_W, NA_KROWS * GRID_W
    sel = lambda g: jnp.where(g == 0, 0, jnp.where(g == n_groups - 1, 2, 1))
    return pl.pallas_call(
        functools.partial(_na_kernel, rows=rows),
        grid=(batch, n_groups),
        in_specs=[pl.BlockSpec((nq, D_NA), lambda b, g: (b * n_groups + g, 0)),
                  pl.BlockSpec((seq, D_NA), lambda b, g: (b, 1)),
                  pl.BlockSpec((seq, NA_HEADS * V_SLOT), lambda b, g: (b, 1)),
                  pl.BlockSpec((ctx_len, D_NA), lambda b, g: (b, 1)),
                  pl.BlockSpec((ctx_len, NA_HEADS * V_SLOT), lambda b, g: (b, 1)),
                  pl.BlockSpec((1, NA_HEADS, nq, nk), lambda b, g: (sel(g), 0, 0, 0))],
        out_specs=pl.BlockSpec((nq, D_NA), lambda b, g: (b * n_groups + g, 0)),
        out_shape=jax.ShapeDtypeStruct((batch * seq, D_NA), BF16),
        compiler_params=_params(("parallel", "arbitrary")),
        name="neighbourhood_attention",
    )(qkv, qkv, qkv, qkv_c, qkv_c, bias)


def _ctx_attn_kernel(q_ref, k_ref, v_ref, o_ref):
    for h in range(NA_HEADS):
        sl = slice(h * NA_HEAD_DIM, (h + 1) * NA_HEAD_DIM)
        o = _softmax_pv([_nt_dot(q_ref[:, sl], k_ref[:, sl])], [v_ref[:, h * V_SLOT:(h + 1) * V_SLOT]])
        o_ref[:, sl] = o.astype(o_ref.dtype)


def _context_attention(qkv_c, batch, ctx_len):
    spec = lambda j: pl.BlockSpec((ctx_len, D_NA), lambda b: (b, j))
    return pl.pallas_call(
        _ctx_attn_kernel,
        grid=(batch,),
        in_specs=[spec(0), spec(1), pl.BlockSpec((ctx_len, NA_HEADS * V_SLOT), lambda b: (b, 1))],
        out_specs=spec(0),
        out_shape=jax.ShapeDtypeStruct((batch * ctx_len, D_NA), BF16),
        compiler_params=_params(("parallel",)),
        name="context_attention",
    )(qkv_c, qkv_c, qkv_c)


SUBLANES = 8
LANES = 128
SHORT_CHUNK = 128
DW_CHUNK = 128
DW_STAGES = 2
HY_HALO = 8
CF_HALO = 16


def _conv_kernel(hy_ref, cf_ref, sw_ref, sb_ref, dw_ref, db_ref, lg_ref, lb_ref,
                 x1_ref, x2_ref, v_ref, cfo_ref, hpad, gpad, ph_scr, *, seq):
    hpad[0:HY_HALO, :] = jnp.zeros((HY_HALO, D_HYIN), F32)
    hpad[seq + HY_HALO:seq + 2 * HY_HALO, :] = jnp.zeros((HY_HALO, D_HYIN), F32)
    hpad[HY_HALO:seq + HY_HALO, :] = hy_ref[...]
    gpad[0:CF_HALO, :] = jnp.zeros((CF_HALO, D_CF), F32)
    gpad[seq + CF_HALO:seq + 2 * CF_HALO, :] = jnp.zeros((CF_HALO, D_CF), F32)
    gpad[CF_HALO:seq + CF_HALO, :] = cf_ref[:, :D_CF] * jax.nn.sigmoid(cf_ref[:, D_CF:])
    pad_s = 1
    pad_c = (CF_KERNEL - 1) // 2

    def short_body(i, carry):
        ch = SHORT_CHUNK
        base = pl.multiple_of(i * ch, ch)
        win = hpad[pl.ds(base, ch + 2 * HY_HALO), :]
        u = sb_ref[...]
        for t in range(3):
            o = HY_HALO - pad_s + t
            u = u + win[o:o + ch, :] * sw_ref[t:t + 1, :]
        x1_ref[i] = u[:, :D_HY].T
        x2_ref[i] = u[:, D_HY:2 * D_HY].T
        v_ref[i] = u[:, 2 * D_HY:].T
        return carry

    lax.fori_loop(0, seq // SHORT_CHUNK, short_body, 0)

    def dw_chunk(base, stage):
        ch = DW_CHUNK
        gw = gpad[pl.ds(base, ch + 2 * CF_HALO), :]
        span = ch + 2 * CF_HALO - SUBLANES
        for phase in range(1, SUBLANES):
            ph_scr[stage, phase - 1] = gw[phase:phase + span, :]
        y = db_ref[...]
        for t in range(CF_KERNEL):
            o = CF_HALO - pad_c + t
            phase = o % SUBLANES
            if phase == 0:
                src = gpad[pl.ds(base + o, ch), :]
            else:
                src = ph_scr[stage, phase - 1, o - phase:o - phase + ch, :]
            y = y + src * dw_ref[t:t + 1, :]
        mu = jnp.mean(y, axis=-1, keepdims=True)
        yc = y - mu
        var = jnp.mean(yc * yc, axis=-1, keepdims=True)
        yn = yc * lax.rsqrt(var + EPS) * lg_ref[...] + lb_ref[...]
        cfo_ref[pl.ds(base, ch), :] = (yn * jax.nn.sigmoid(yn)).astype(cfo_ref.dtype)

    def dw_body(i, carry):
        for stage in range(DW_STAGES):
            dw_chunk(pl.multiple_of((i * DW_STAGES + stage) * DW_CHUNK, DW_CHUNK), stage)
        return carry

    lax.fori_loop(0, seq // (DW_CHUNK * DW_STAGES), dw_body, 0)


def _conv_mixers(hy_in, cf_in, sw, sb, dw, db, lg, lb, batch, seq):
    full = lambda a: pl.BlockSpec(a.shape, lambda b: (0, 0))
    cm = pl.BlockSpec((seq // LANES, None, D_HY, LANES), lambda b: (0, b, 0, 0))
    cm_shape = jax.ShapeDtypeStruct((seq // LANES, batch, D_HY, LANES), F32)
    return pl.pallas_call(
        functools.partial(_conv_kernel, seq=seq),
        grid=(batch,),
        in_specs=[pl.BlockSpec((seq, D_HYIN), lambda b: (b, 0)), pl.BlockSpec((seq, D_CFIN), lambda b: (b, 0)),
                  full(sw), full(sb), full(dw), full(db), full(lg), full(lb)],
        out_specs=[cm, cm, cm, pl.BlockSpec((seq, D_CF), lambda b: (b, 0))],
        out_shape=[cm_shape, cm_shape, cm_shape, jax.ShapeDtypeStruct((batch * seq, D_CF), BF16)],
        scratch_shapes=[pltpu.VMEM((seq + 2 * HY_HALO, D_HYIN), F32), pltpu.VMEM((seq + 2 * CF_HALO, D_CF), F32),
                        pltpu.VMEM((DW_STAGES, SUBLANES - 1, DW_CHUNK + 2 * CF_HALO - SUBLANES, D_CF), F32)],
        compiler_params=_params(("parallel",)),
        name="conv_mixers",
    )(hy_in, cf_in, sw, sb, dw, db, lg, lb)


def _filter_kernel(fb_ref, w1t_ref, w1c_ref, w1s_ref, b1_ref, fr_ref, w2_ref, b2_ref, w3_ref, rate_ref, o_ref, *, seq):
    pos = lax.broadcasted_iota(I32, (seq, 1), 0).astype(F32)
    t = pos / max(seq - 1, 1)
    ang = fb_ref[...] * (2.0 * math.pi) * pos / seq
    dot = functools.partial(jnp.dot, preferred_element_type=F32, precision=HIGHEST)
    fr = fr_ref[...]
    z1 = t * w1t_ref[...] + dot(jnp.cos(ang), w1c_ref[...]) + dot(-jnp.sin(ang), w1s_ref[...]) + b1_ref[...]
    h = jnp.sin(fr * z1)
    h = jnp.sin(fr * (dot(h, w2_ref[...]) + b2_ref[...]))
    o_ref[...] = dot(h, w3_ref[...]) * jnp.exp(-t * rate_ref[...])


def _hyena_filters(seq, w1, b1, freq, w2, b2, w3):
    bands = (HY_EMB - 1) // 2
    fb = jnp.linspace(1e-4, bands - 1, bands, dtype=F32)[None, :]
    min_decay = math.log(HY_TARGET) / HY_SLOW_DECAY
    max_decay = math.log(HY_TARGET) / HY_FAST_DECAY
    rate = jnp.abs(jnp.linspace(min_decay, max_decay, D_HY, dtype=F32))
    rate4 = jnp.tile(rate, 4)[None, :]
    args = (fb, w1[0:1], w1[1:1 + bands], w1[1 + bands:], b1[None, :], freq[None, :], w2, b2[None, :], w3, rate4)
    h = pl.pallas_call(
        functools.partial(_filter_kernel, seq=seq),
        out_shape=jax.ShapeDtypeStruct((seq, 4 * D_HY), F32),
        compiler_params=_params(None),
        name="hyena_filter",
    )(*args)
    h = h.reshape(seq, 2, 2, D_HY)
    fwd, bwd = h[:, 0], h[:, 1]
    line = jnp.concatenate([jnp.zeros_like(fwd[:1]), bwd[1:][::-1], fwd], axis=0)
    return jnp.transpose(line, (1, 2, 0))


HY_CG = 8


def _hyena_kernel(d_ref, x1_ref, x2_ref, v_ref, kf_ref, o_ref, w_scr, acc_scr, *, batch, seq):
    nb = seq // HY_T
    tiles = HY_T // LANES
    cg = pl.program_id(0)

    def rows_of(ref):
        return ref.reshape(seq // LANES * batch * HY_CG, LANES)

    def tile_rows(k, c):
        return pl.ds((k * batch) * HY_CG + c, batch, stride=HY_CG)

    def blocks(ref, c):
        rows = rows_of(ref)
        return jnp.concatenate(
            [jnp.concatenate([rows[tile_rows(j * tiles + h, c), :] for h in range(tiles)], axis=1)
             for j in range(nb)], axis=0)

    out_rows = rows_of(o_ref)
    for c in range(HY_CG):
        z = blocks(v_ref, c)
        for n, gate_ref in enumerate((x1_ref, x2_ref)):
            line = jnp.broadcast_to(kf_ref[n, c:c + 1, :], (HY_T, 2 * seq))
            w_scr[...] = pltpu.roll(line, 0, 1, stride=1, stride_axis=0).astype(BF16)
            zb = z.astype(BF16)
            acc_scr[...] = jnp.dot(zb, w_scr[:, seq:seq + HY_T], preferred_element_type=F32)
            for d in range(1, nb):
                m = batch * (nb - d)
                acc_scr[batch * d:, :] += jnp.dot(zb[:m], w_scr[:, seq + d * HY_T:seq + (d + 1) * HY_T],
                                                  preferred_element_type=F32)
                acc_scr[:m, :] += jnp.dot(zb[batch * d:], w_scr[:, seq - d * HY_T:seq - (d - 1) * HY_T],
                                          preferred_element_type=F32)
            z = blocks(gate_ref, c) * (acc_scr[...] + d_ref[n, cg * HY_CG + c] * z)
        for j in range(nb):
            for h in range(tiles):
                out_rows[tile_rows(j * tiles + h, c), :] = z[j * batch:(j + 1) * batch, h * LANES:(h + 1) * LANES]


def _hyena_long_conv(x1t, x2t, vt, kf, bias_d, batch, seq):
    cm = pl.BlockSpec((seq // LANES, batch, HY_CG, LANES), lambda g: (0, 0, g, 0))
    return pl.pallas_call(
        functools.partial(_hyena_kernel, batch=batch, seq=seq),
        grid=(D_HY // HY_CG,),
        in_specs=[pl.BlockSpec(memory_space=pltpu.SMEM), cm, cm, cm,
                  pl.BlockSpec((2, HY_CG, 2 * seq), lambda g: (0, g, 0))],
        out_specs=cm,
        out_shape=jax.ShapeDtypeStruct((seq // LANES, batch, D_HY, LANES), F32),
        scratch_shapes=[pltpu.VMEM((HY_T, 2 * seq), BF16), pltpu.VMEM((seq // HY_T * batch, HY_T), F32)],
        compiler_params=_params(("parallel",)),
        name="hyena_long_conv",
    )(bias_d, x1t, x2t, vt, kf)


def _outproj_kernel(attn_ref, hy_ref, cf_ref, x_ref, g1_ref, w_ref, ng_ref, sc_ref, sh_ref, rw_ref,
                    xo_ref, h2_ref, lt_ref):
    y = jnp.dot(attn_ref[...], w_ref[:D_NA], preferred_element_type=F32)
    hy = jnp.concatenate([hy_ref[k].T for k in range(hy_ref.shape[0])], axis=0).astype(BF16)
    y = y + jnp.dot(hy, w_ref[D_NA:D_NA + D_HY], preferred_element_type=F32)
    y = y + jnp.dot(cf_ref[...], w_ref[D_NA + D_HY:], preferred_element_type=F32)
    xn = x_ref[...] + g1_ref[0] * y
    xo_ref[...] = xn
    h2 = _rms_mod(xn, ng_ref[...], sc_ref[0], sh_ref[0]).astype(BF16)
    h2_ref[...] = h2
    lt_ref[0] = _nt_dot(rw_ref[...], h2)


def _out_projection(attn, hyt, cf, x2d, g1, w_bf, ng, sc, sh, rw_t, batch, seq, tm):
    n = x2d.shape[0]
    tps = seq // tm
    row = lambda w: pl.BlockSpec((tm, w), lambda i: (i, 0))
    full = lambda a: pl.BlockSpec(a.shape, lambda i: (0, 0))
    return pl.pallas_call(
        _outproj_kernel,
        grid=(n // tm,),
        in_specs=[row(D_NA), pl.BlockSpec((tm // LANES, None, D_HY, LANES), lambda i: (i % tps, i // tps, 0, 0)),
                  row(D_CF), row(D_MODEL),
                  _per_sample_spec(g1, tps), full(w_bf), full(ng), _per_sample_spec(sc, tps),
                  _per_sample_spec(sh, tps), full(rw_t)],
        out_specs=[row(D_MODEL), row(D_MODEL), pl.BlockSpec((1, N_EXPERTS, tm), lambda i: (i // tps, 0, i % tps))],
        out_shape=[jax.ShapeDtypeStruct((n, D_MODEL), F32), jax.ShapeDtypeStruct((n, D_MODEL), BF16),
                   jax.ShapeDtypeStruct((batch, N_EXPERTS, seq), F32)],
        compiler_params=_params(("parallel",)),
        name="out_projection",
    )(attn, hyt, cf, x2d, g1, w_bf, ng, sc, sh, rw_t)


def _route_kernel(lt_ref, tri_ref, tcol_ref, pos_ref, aff_ref, cnt_ref, *, cap):
    lg = lt_ref[...]
    bs, _, seq = lg.shape
    e = jnp.exp(lg - jnp.max(lg, axis=1, keepdims=True))
    aff = e / jnp.sum(e, axis=1, keepdims=True)
    aff_ref[...] = aff
    keys = pltpu.bitcast(aff.reshape(bs * N_EXPERTS, seq), I32)

    def body(i, thr):
        cand = thr | (jnp.int32(1) << (30 - i))
        cnt = jnp.sum(jnp.where(keys >= cand, 1.0, 0.0), axis=1, keepdims=True)
        return jnp.where(cnt >= cap, cand, thr)

    thr = lax.fori_loop(0, 31, body, jnp.zeros((bs * N_EXPERTS, 1), I32))
    gt = keys > thr
    eq = keys == thr
    need = cap - jnp.sum(jnp.where(gt, 1.0, 0.0), axis=1, keepdims=True)
    rank = jnp.dot(jnp.where(eq, 1.0, 0.0).astype(BF16), tri_ref[...], preferred_element_type=F32)
    sel = gt | (eq & (rank < need))
    sel_bf = jnp.where(sel, 1.0, 0.0).astype(BF16)
    slot = jnp.dot(sel_bf, tri_ref[...], preferred_element_type=F32)
    pos_ref[...] = jnp.where(sel, slot, -1.0).astype(I32).reshape(bs, N_EXPERTS, seq)
    cnt = jnp.dot(sel_bf, tcol_ref[...], preferred_element_type=F32)
    cnt_ref[...] = cnt.astype(I32).reshape(bs, N_EXPERTS, LANES)


def _routing(logits_t, cap, tl):
    batch, _, seq = logits_t.shape
    bs = 8 if batch % 8 == 0 else batch
    idx = np.arange(seq)
    tri = jnp.asarray(idx[:, None] < idx[None, :], dtype=BF16)
    tile_start = np.minimum(np.arange(LANES) * tl, seq)
    tcol = jnp.asarray(idx[:, None] < tile_start[None, :], dtype=BF16)
    blk = pl.BlockSpec((bs, N_EXPERTS, seq), lambda i: (i, 0, 0))
    cblk = pl.BlockSpec((bs, N_EXPERTS, LANES), lambda i: (i, 0, 0))
    pos, aff, cnt = pl.pallas_call(
        functools.partial(_route_kernel, cap=cap),
        grid=(batch // bs,),
        in_specs=[blk, pl.BlockSpec((seq, seq), lambda i: (0, 0)), pl.BlockSpec((seq, LANES), lambda i: (0, 0))],
        out_specs=[blk, blk, cblk],
        out_shape=[jax.ShapeDtypeStruct((batch, N_EXPERTS, seq), I32),
                   jax.ShapeDtypeStruct((batch, N_EXPERTS, seq), F32),
                   jax.ShapeDtypeStruct((batch, N_EXPERTS, LANES), I32)],
        compiler_params=_params(("parallel",)),
        name="ec_routing",
    )(logits_t, tri, tcol)
    return pos, aff, cnt[:, :, :seq // tl]


ROUTE_TILE = 512
SLOT_ALIGN = 16
GATHER_GROUP = 4


def _slot_window_size(cap, n_tiles):
    if n_tiles == 1:
        return cap
    share = cap // n_tiles
    return min(cap, -(-(share + share // 2 + SLOT_ALIGN) // SLOT_ALIGN) * SLOT_ALIGN)


def _slot_window(cnt_ref, sample, expert, tile, n_tiles, cap, window):
    base = (sample * N_EXPERTS + expert) * n_tiles + tile
    lo = cnt_ref[base]
    hi = cnt_ref[base + 1] if tile + 1 < n_tiles else cap
    start = jnp.minimum(lo // SLOT_ALIGN * SLOT_ALIGN, cap - window)
    return pl.multiple_of(start, SLOT_ALIGN), hi - start <= window


def _gather_kernel(cnt_ref, pos_ref, h_ref, xe_ref, oh_scr, *, cap, window, tl):
    b = pl.program_id(0)
    seq = h_ref.shape[0]
    n_t = seq // tl

    def gather_all_tokens():
        slot = lax.broadcasted_iota(I32, (cap, seq), 0)
        for e in range(N_EXPERTS):
            onehot = jnp.where(pos_ref[0, e:e + 1, :] == slot, 1.0, 0.0).astype(BF16)
            xe_ref[0, e * cap:(e + 1) * cap, :] = jnp.dot(onehot, h_ref[...],
                                                          preferred_element_type=F32).astype(BF16)

    if window == cap:
        gather_all_tokens()
        return

    starts, fits = {}, True
    for e in range(N_EXPERTS):
        for t in range(n_t):
            starts[e, t], ok = _slot_window(cnt_ref, b, e, t, n_t, cap, window)
            fits = jnp.logical_and(fits, ok)

    @pl.when(fits)
    def _():
        xe_ref[...] = jnp.zeros(xe_ref.shape, BF16)
        slot = lax.broadcasted_iota(I32, (window, tl), 0)
        for t in range(n_t):
            for g in range(N_EXPERTS // GATHER_GROUP):
                for i in range(GATHER_GROUP):
                    e = g * GATHER_GROUP + i
                    hit = pos_ref[0, e:e + 1, t * tl:(t + 1) * tl] == slot + starts[e, t]
                    oh_scr[i * window:(i + 1) * window, :] = jnp.where(hit, 1.0, 0.0).astype(BF16)
                rows = jnp.dot(oh_scr[...], h_ref[t * tl:(t + 1) * tl, :], preferred_element_type=F32).astype(BF16)
                for i in range(GATHER_GROUP):
                    e = g * GATHER_GROUP + i
                    row0 = pl.multiple_of(e * cap + starts[e, t], SLOT_ALIGN)
                    xe_ref[0, pl.ds(row0, window), :] += rows[i * window:(i + 1) * window]

    @pl.when(jnp.logical_not(fits))
    def _():
        gather_all_tokens()


def _gather_tokens(pos, cnt, h2, cap, tl):
    batch, _, seq = pos.shape
    window = _slot_window_size(cap, seq // tl)
    grid_spec = pltpu.PrefetchScalarGridSpec(
        num_scalar_prefetch=1,
        grid=(batch,),
        in_specs=[pl.BlockSpec((1, N_EXPERTS, seq), lambda b, c: (b, 0, 0)),
                  pl.BlockSpec((seq, D_MODEL), lambda b, c: (b, 0))],
        out_specs=pl.BlockSpec((1, N_EXPERTS * cap, D_MODEL), lambda b, c: (b, 0, 0)),
        scratch_shapes=[pltpu.VMEM((GATHER_GROUP * window, tl), BF16)])
    return pl.pallas_call(
        functools.partial(_gather_kernel, cap=cap, window=window, tl=tl),
        grid_spec=grid_spec,
        out_shape=jax.ShapeDtypeStruct((batch, N_EXPERTS * cap, D_MODEL), BF16),
        compiler_params=_params(("parallel",)),
        name="ec_gather",
    )(cnt.reshape(-1), pos, h2)


FFN_ROWS = 512


def _ffn_kernel(*refs, layer, n_ctx_tiles):
    if n_ctx_tiles:
        xe_ref, xc_ref, w1_hbm, w3_hbm, w2_hbm, y_ref, yc_ref, w1b, w3b, w2b, st1, st3, st2, sem = refs
    else:
        xe_ref, w1_hbm, w3_hbm, w2_hbm, y_ref, w1b, w3b, w2b, st1, st3, st2, sem = refs
    e, rt = pl.program_id(0), pl.program_id(1)
    n_e, n_rt = N_EXPERTS, pl.num_programs(1)
    r1, r2 = st1.shape[1], st2.shape[1]

    def slab_copies(expert, k, slot):
        o1, o2 = pl.multiple_of(k * r1, r1), pl.multiple_of(k * r2, r2)
        return (pltpu.make_async_copy(w1_hbm.at[layer, expert, pl.ds(o1, r1), :], st1.at[slot], sem.at[slot, 0]),
                pltpu.make_async_copy(w3_hbm.at[layer, expert, pl.ds(o1, r1), :], st3.at[slot], sem.at[slot, 1]),
                pltpu.make_async_copy(w2_hbm.at[layer, expert, pl.ds(o2, r2), :], st2.at[slot], sem.at[slot, 2]))

    def start_slab(expert, k, slot):
        for cp in slab_copies(expert, k, slot):
            cp.start()

    def cast_slab(k, slot, wslot):
        o1, o2 = pl.multiple_of(k * r1, r1), pl.multiple_of(k * r2, r2)
        w1b[wslot, pl.ds(o1, r1), :] = st1[slot].astype(BF16)
        w3b[wslot, pl.ds(o1, r1), :] = st3[slot].astype(BF16)
        w2b[wslot, pl.ds(o2, r2), :] = st2[slot].astype(BF16)

    def land_slab(expert, k, slot, wslot):
        for cp in slab_copies(expert, k, slot):
            cp.wait()
        cast_slab(k, slot, wslot)

    def prefetched_expert(expert):
        return jnp.minimum(expert + 1, n_e - 1)

    n_slabs = D_MODEL // r1

    @pl.when((e == 0) & (rt == 0))
    def _():
        start_slab(0, 0, 0)
        for k in range(n_slabs):
            if k + 1 < n_slabs:
                start_slab(0, k + 1, (k + 1) % 2)
            land_slab(0, k, k % 2, 0)
        start_slab(prefetched_expert(0), 0, 0)

    slot = (e * n_rt + rt) % 2
    for cp in slab_copies(prefetched_expert(e), rt, slot):
        cp.wait()
    last_tile = rt + 1 == n_rt
    nxt_e = jnp.minimum(jnp.where(last_tile, e + 1, e), n_e - 1)
    nxt_k = jnp.where(last_tile, 0, rt + 1)
    start_slab(prefetched_expert(nxt_e), nxt_k, 1 - slot)
    cast_slab(rt, slot, (e + 1) % 2)
    ws = e % 2

    def swiglu(x_ref, o_ref):
        bt, cap, _ = x_ref.shape
        x = x_ref[...].reshape(bt * cap, D_MODEL)
        a = jnp.dot(x, w1b[ws], preferred_element_type=F32)
        u = jnp.dot(x, w3b[ws], preferred_element_type=F32)
        h = (a * jax.nn.sigmoid(a) * u).astype(BF16)
        o_ref[...] = jnp.dot(h, w2b[ws], preferred_element_type=F32).astype(BF16).reshape(bt, cap, D_MODEL)

    swiglu(xe_ref, y_ref)
    if n_ctx_tiles:
        @pl.when(rt < n_ctx_tiles)
        def _():
            swiglu(xc_ref, yc_ref)

    @pl.when((e == n_e - 1) & last_tile)
    def _():
        for cp in slab_copies(prefetched_expert(nxt_e), nxt_k, 1 - slot):
            cp.wait()


def _tile_samples(batch, cap):
    bt = max(1, min(batch, FFN_ROWS // cap))
    while batch % bt:
        bt -= 1
    return bt


def _expert_ffn(xe, xe_ctx, w1, w3, w2, layer):
    batch = xe.shape[0]
    cap = xe.shape[1] // N_EXPERTS
    bt = _tile_samples(batch, cap)
    n_rt = batch // bt
    xspec = pl.BlockSpec((bt, cap, D_MODEL), lambda e, r: (r, e, 0))
    hbm = pl.BlockSpec(memory_space=pl.ANY)
    in_specs, out_specs, out_shape, args, n_ctx_tiles = [xspec], [xspec], [jax.ShapeDtypeStruct(xe.shape, BF16)], [xe], 0
    if xe_ctx is not None:
        cap_c = xe_ctx.shape[1] // N_EXPERTS
        bt_c = _tile_samples(batch, cap_c)
        n_ctx_tiles = batch // bt_c
        assert n_ctx_tiles <= n_rt
        cspec = pl.BlockSpec((bt_c, cap_c, D_MODEL), lambda e, r: (jnp.minimum(r, n_ctx_tiles - 1), e, 0))
        in_specs.append(cspec)
        out_specs.append(cspec)
        out_shape.append(jax.ShapeDtypeStruct(xe_ctx.shape, BF16))
        args.append(xe_ctx)
    assert D_MODEL % n_rt == 0 and EXPERT_FF % n_rt == 0
    r1, r2 = D_MODEL // n_rt, EXPERT_FF // n_rt
    out = pl.pallas_call(
        functools.partial(_ffn_kernel, layer=layer, n_ctx_tiles=n_ctx_tiles),
        grid=(N_EXPERTS, n_rt),
        in_specs=in_specs + [hbm, hbm, hbm],
        out_specs=out_specs,
        out_shape=out_shape,
        scratch_shapes=[pltpu.VMEM((2, D_MODEL, EXPERT_FF), BF16), pltpu.VMEM((2, D_MODEL, EXPERT_FF), BF16),
                        pltpu.VMEM((2, EXPERT_FF, D_MODEL), BF16), pltpu.VMEM((2, r1, EXPERT_FF), F32),
                        pltpu.VMEM((2, r1, EXPERT_FF), F32), pltpu.VMEM((2, r2, D_MODEL), F32),
                        pltpu.SemaphoreType.DMA((2, 3))],
        compiler_params=_params(("arbitrary", "arbitrary")),
        name="ec_expert_ffn",
    )(*args, w1, w3, w2)
    return (out[0], out[1]) if xe_ctx is not None else (out[0], None)


def _scatter_kernel(cnt_ref, pos_ref, aff_ref, y_ref, x_ref, g2_ref, fg_ref, o_ref, pg_scr, yw_scr, *,
                    cap, window, final_norm):
    b, t = pl.program_id(0), pl.program_id(1)
    n_t = pl.num_programs(1)
    tl = x_ref.shape[0]

    def finish(moe):
        xn = x_ref[...] + g2_ref[0] * moe
        if final_norm:
            xn = xn * lax.rsqrt(jnp.mean(xn * xn, axis=-1, keepdims=True) + EPS) * fg_ref[...]
        o_ref[...] = xn

    def scatter_all_slots():
        slot = lax.broadcasted_iota(I32, (cap, tl), 0)
        for e in range(N_EXPERTS):
            hit = pos_ref[0, e:e + 1, :] == slot
            pg_scr[e * cap:(e + 1) * cap, :] = jnp.where(hit, aff_ref[0, e:e + 1, :], 0.0).astype(BF16)
        finish(_tn_dot(pg_scr[...], y_ref[0]))

    if window == cap:
        scatter_all_slots()
        return

    starts, fits = [], True
    for e in range(N_EXPERTS):
        base = (b * N_EXPERTS + e) * n_t + t
        lo = cnt_ref[base]
        hi = jnp.where(t + 1 < n_t, cnt_ref[jnp.minimum(base + 1, cnt_ref.shape[0] - 1)], cap)
        start = jnp.minimum(lo // SLOT_ALIGN * SLOT_ALIGN, cap - window)
        starts.append(pl.multiple_of(start, SLOT_ALIGN))
        fits = jnp.logical_and(fits, hi - start <= window)

    @pl.when(fits)
    def _():
        slot = lax.broadcasted_iota(I32, (window, tl), 0)
        for e in range(N_EXPERTS):
            hit = pos_ref[0, e:e + 1, :] == slot + starts[e]
            pg_scr[e * window:(e + 1) * window, :] = jnp.where(hit, aff_ref[0, e:e + 1, :], 0.0).astype(BF16)
            row0 = pl.multiple_of(e * cap + starts[e], SLOT_ALIGN)
            yw_scr[e * window:(e + 1) * window, :] = y_ref[0, pl.ds(row0, window), :]
        finish(_tn_dot(pg_scr[:N_EXPERTS * window, :], yw_scr[...]))

    @pl.when(jnp.logical_not(fits))
    def _():
        scatter_all_slots()


def _scatter_residual(pos, aff, cnt, y, x2d, g2, fg, cap, tl, final_norm):
    batch, _, seq = pos.shape
    tps = seq // tl
    window = _slot_window_size(cap, tps)
    rspec =pl.BlockSpec((1, N_EXPERTS, tl), lambda b, t, c: (b, 0, t))
    xspec = pl.BlockSpec((tl, D_MODEL), lambda b, t, c: (b * tps + t, 0))
    g2spec = (pl.BlockSpec((1, 1, D_MODEL), lambda b, t, c: (0, 0, 0)) if g2.shape[0] == 1
              else pl.BlockSpec((1, 1, D_MODEL), lambda b, t, c: (b, 0, 0)))
    grid_spec = pltpu.PrefetchScalarGridSpec(
        num_scalar_prefetch=1,
        grid=(batch, tps),
        in_specs=[rspec, rspec, pl.BlockSpec((1, N_EXPERTS * cap, D_MODEL), lambda b, t, c: (b, 0, 0)), xspec,
                  g2spec, pl.BlockSpec((1, D_MODEL), lambda b, t, c: (0, 0))],
        out_specs=xspec,
        scratch_shapes=[pltpu.VMEM((N_EXPERTS * cap, tl), BF16), pltpu.VMEM((N_EXPERTS * window, D_MODEL), BF16)])
    return pl.pallas_call(
        functools.partial(_scatter_kernel, cap=cap, window=window, final_norm=final_norm),
        grid_spec=grid_spec,
        out_shape=jax.ShapeDtypeStruct(x2d.shape, F32),
        compiler_params=_params(("parallel", "arbitrary")),
        name="ec_scatter_residual",
    )(cnt.reshape(-1), pos, aff, y, x2d, g2, fg)


def _stream_front(x2d, mods, lw, batch, seq, tm, attn_fn):
    sh1, sc1, g1, sh2, sc2, g2 = mods
    qkv, hy_in, cf_in = _in_projection(x2d, lw["norm1_g"], sc1, sh1, lw["w_in"], seq, tm)
    attn = attn_fn(qkv)
    x1t, x2t, vt, cf = _conv_mixers(hy_in, cf_in, lw["hy_short_w"], lw["hy_short_b"], lw["cf_dw_w"], lw["cf_dw_b"],
                                    lw["cf_ln_g"], lw["cf_ln_b"], batch, seq)
    kf = _hyena_filters(seq, lw["hy_filt_w1"], lw["hy_filt_b1"], lw["hy_filt_freq"], lw["hy_filt_w2"],
                        lw["hy_filt_b2"], lw["hy_filt_w3"])
    hyt = _hyena_long_conv(x1t, x2t, vt, kf, lw["hy_bias_d"], batch, seq)
    x_mid, h2, logits_t = _out_projection(attn, hyt, cf, x2d, g1, lw["w_out"], lw["norm2_g"], sc2, sh2,
                                          lw["router_wt"], batch, seq, tm)
    cap = EC_CAPACITY * seq // N_EXPERTS
    pos, aff, cnt = _routing(logits_t, cap, min(seq, ROUTE_TILE))
    xe = _gather_tokens(pos, cnt, h2, cap, min(seq, ROUTE_TILE))
    return dict(pos=pos, aff=aff, cnt=cnt, xe=xe, x_mid=x_mid, g2=g2, cap=cap, seq=seq, qkv=qkv)


def _stream_back(front, y, lw, final_norm):
    return _scatter_residual(front["pos"], front["aff"], front["cnt"], y, front["x_mid"], front["g2"],
                             lw["final_norm_g"], front["cap"], min(front["seq"], ROUTE_TILE), final_norm)


def kernel(x, c, ctx, c_ctx, w_mod, b_mod, norm1_g, norm2_g, w_in, na_rpb, hy_short_w, hy_short_b, hy_filt_w1,
           hy_filt_b1, hy_filt_freq, hy_filt_w2, hy_filt_b2, hy_filt_w3, hy_bias_d, cf_dw_w, cf_dw_b, cf_ln_g,
           cf_ln_b, w_out, router_w, expert_w1, expert_w3, expert_w2, final_norm_g):
    batch, seq, _ = x.shape
    ctx_len = ctx.shape[1]
    depth = w_mod.shape[0]
    rows = seq // GRID_W

    n_c = batch + 1
    n_c_pad = -(-n_c // 8) * 8
    cvec = jnp.concatenate([c, c_ctx[None, :], jnp.zeros((n_c_pad - n_c, D_MODEL), F32)], axis=0)
    mod_all = _modulation(cvec, w_mod, b_mod)

    xl = x.reshape(batch * seq, D_MODEL)
    xc = ctx.reshape(batch * ctx_len, D_MODEL)
    for l in range(depth):
        last = l == depth - 1
        lw = dict(
            norm1_g=norm1_g[l][None, :], norm2_g=norm2_g[l][None, :], w_in=w_in[l].astype(BF16),
            hy_short_w=hy_short_w[l], hy_short_b=hy_short_b[l][None, :], hy_filt_w1=hy_filt_w1[l],
            hy_filt_b1=hy_filt_b1[l], hy_filt_freq=hy_filt_freq[l], hy_filt_w2=hy_filt_w2[l],
            hy_filt_b2=hy_filt_b2[l], hy_filt_w3=hy_filt_w3[l], hy_bias_d=hy_bias_d[l], cf_dw_w=cf_dw_w[l],
            cf_dw_b=cf_dw_b[l][None, :], cf_ln_g=cf_ln_g[l][None, :], cf_ln_b=cf_ln_b[l][None, :],
            w_out=w_out[l].astype(BF16), router_wt=router_w[l].T.astype(BF16),
            expert_w1=expert_w1, expert_w3=expert_w3, expert_w2=expert_w2, layer=l,
            final_norm_g=final_norm_g[None, :])
        chunks = [mod_all[l, :, j * D_MODEL:(j + 1) * D_MODEL] for j in range(N_MOD)]
        mods_l = [m[:batch, None, :] for m in chunks]
        mods_c = [m[batch:batch + 1, None, :] for m in chunks]
        bias = _na_bias_tiles(na_rpb[l], rows)

        if last:
            front_c = None
            qkv_c, _, _ = _in_projection(xc, lw["norm1_g"], mods_c[1], mods_c[0], lw["w_in"], ctx_len, ctx_len)
        else:
            front_c = _stream_front(xc, mods_c, lw, batch, ctx_len, ctx_len,
                                    lambda qkv: _context_attention(qkv, batch, ctx_len))
            qkv_c = front_c["qkv"]
        front_l = _stream_front(xl, mods_l, lw, batch, seq, 512,
                                lambda qkv: _neighbourhood_attention(qkv, qkv_c, bias, batch, seq, ctx_len))
        y_l, y_c = _expert_ffn(front_l["xe"], None if last else front_c["xe"], expert_w1, expert_w3, expert_w2, l)
        if not last:
            xc = _stream_back(front_c, y_c, lw, False)
        xl = _stream_back(front_l, y_l, lw, last)
    return xl.reshape(batch, seq, D_MODEL)
```
